```python
import jax, jax.numpy as jnp
from jax import lax
import numpy as np

D_MODEL = 2048
BATCH = 2
SEQ = 16384
DEPTH = 2

CHUNK = 64
Q_BLOCK = 128
HEAD_DIM = 128
N_HEADS = 4
BRANCH_W = N_HEADS * HEAD_DIM
N_BRANCH = 3
LEFT_CHUNKS = 8
BAND = (LEFT_CHUNKS + 1) * CHUNK
MAX_REL = 128
N_REL = 2 * MAX_REL + 1
QKVG_COLS = N_BRANCH * 4 * BRANCH_W
IN_COLS = QKVG_COLS + N_HEADS + N_BRANCH * D_MODEL
EPS = 1e-6
NEG = -1e30

kernel_name = "hybrid_fox_chunkrel_stickbreak_gated"


def rmsnorm(x, g):
    xf = x.astype(jnp.float32)
    y = xf * lax.rsqrt(jnp.mean(xf * xf, axis=-1, keepdims=True) + EPS)
    return (y * g.astype(jnp.float32)).astype(x.dtype)


def to_heads(t):
    b, s, _ = t.shape
    return t.reshape(b, s, N_HEADS, HEAD_DIM).transpose(0, 2, 1, 3)


def from_heads(o):
    b, h, s, d = o.shape
    return o.transpose(0, 2, 1, 3).reshape(b, s, h * d)


def to_blocks(t, size):
    b, h, s, d = t.shape
    return t.reshape(b, h, s // size, size, d).transpose(2, 0, 1, 3, 4)


def from_blocks(o):
    nb, b, h, size, d = o.shape
    return o.transpose(1, 2, 0, 3, 4).reshape(b, h, nb * size, d)


def forgetting_attention(q, k, v, f_logit):
    b, h, s, d = q.shape
    nb = s // Q_BLOCK
    scale = d ** -0.5
    log_f = jax.nn.log_sigmoid(f_logit.astype(jnp.float32))
    c = jnp.cumsum(log_f, axis=1).transpose(0, 2, 1)
    outs = []
    for i in range(nb):
        lo, hi = i * Q_BLOCK, (i + 1) * Q_BLOCK
        logits = jnp.einsum('bhqd,bhkd->bhqk', q[:, :, lo:hi], k[:, :, :hi]).astype(jnp.float32) * scale
        logits = logits + c[:, :, lo:hi, None] - c[:, :, None, :hi]
        mask = jnp.arange(hi)[None, :] <= jnp.arange(lo, hi)[:, None]
        logits = jnp.where(mask, logits, NEG)
        p = jax.nn.softmax(logits, axis=-1).astype(v.dtype)
        outs.append(jnp.einsum('bhqk,bhkd->bhqd', p, v[:, :, :hi]))
    return jnp.concatenate(outs, axis=2)


def chunked_relpos_attention(q, k, v, rel_bias):
    b, h, s, d = q.shape
    nc = s // CHUNK
    scale = d ** -0.5
    pad = LEFT_CHUNKS * CHUNK
    kp = jnp.pad(k, ((0, 0), (0, 0), (pad, 0), (0, 0)))
    vp = jnp.pad(v, ((0, 0), (0, 0), (pad, 0), (0, 0)))
    qi_pos = jnp.arange(CHUNK)
    band_pos = jnp.arange(BAND)
    rel = pad + qi_pos[:, None] - band_pos[None, :]
    idx = jnp.clip(rel, -MAX_REL, MAX_REL) + MAX_REL
    bias = rel_bias[:, idx].astype(jnp.float32)

    def chunk(args):
        qc, ci = args
        start = ci * CHUNK
        kb = lax.dynamic_slice_in_dim(kp, start, BAND, axis=2)
        vb = lax.dynamic_slice_in_dim(vp, start, BAND, axis=2)
        logits = jnp.einsum('bhqd,bhkd->bhqk', qc, kb).astype(jnp.float32) * scale + bias
        valid = (start - pad + band_pos) >= 0
        logits = jnp.where(valid, logits, NEG)
        p = jax.nn.softmax(logits, axis=-1).astype(v.dtype)
        return jnp.einsum('bhqk,bhkd->bhqd', p, vb)

    o = lax.map(chunk, (to_blocks(q, CHUNK), jnp.arange(nc)))
    return from_blocks(o)


def stick_breaking_attention(q, k, v):
    b, h, s, d = q.shape
    nb = s // Q_BLOCK
    scale = d ** -0.5
    ar = jnp.arange(Q_BLOCK)
    upper = (ar[:, None] >= ar[None, :]).astype(jnp.float32)
    hp = lax.Precision.HIGHEST
    outs = []
    for i in range(nb):
        nk = i + 1
        lo, hi = i * Q_BLOCK, nk * Q_BLOCK
        z = jnp.einsum('bhqd,bhkd->bhqk', q[:, :, lo:hi], k[:, :, :hi]).astype(jnp.float32) * scale
        mask = jnp.arange(hi)[None, :] < jnp.arange(lo, hi)[:, None]
        lk = jnp.where(mask, jax.nn.log_sigmoid(-z), 0.0).reshape(b, h, Q_BLOCK, nk, Q_BLOCK)
        r_in = jnp.einsum('bhqnk,kj->bhqnj', lk, upper, precision=hp)
        tot = r_in[..., 0]
        nbk = jnp.arange(nk)
        later_m = (nbk[:, None] > nbk[None, :]).astype(jnp.float32)
        later = jnp.einsum('bhqm,mn->bhqn', tot, later_m, precision=hp)
        r = (r_in + later[..., None]).reshape(b, h, Q_BLOCK, hi)
        a = jnp.where(mask, jnp.exp(z + r), 0.0).astype(v.dtype)
        outs.append(jnp.einsum('bhqk,bhkd->bhqd', a, v[:, :, :hi]))
    return jnp.concatenate(outs, axis=2)


def hybrid_layer(x, norm_g, w_in, b_f, b_gate, qk_norm_g, rel_bias, w_up, w_out):
    b, s, _ = x.shape
    hn = rmsnorm(x, norm_g)
    proj = jnp.einsum('bsd,dc->bsc', hn, w_in)
    br = proj[..., :QKVG_COLS].reshape(b, s, N_BRANCH, 4, BRANCH_W)
    f_logit = proj[..., QKVG_COLS:QKVG_COLS + N_HEADS] + b_f
    gates = jax.nn.sigmoid(proj[..., QKVG_COLS + N_HEADS:] + b_gate).reshape(b, s, N_BRANCH, D_MODEL)

    qa = rmsnorm(to_heads(br[:, :, 0, 0]), qk_norm_g[0])
    ka = rmsnorm(to_heads(br[:, :, 0, 1]), qk_norm_g[1])
    ya = forgetting_attention(qa, ka, to_heads(br[:, :, 0, 2]), f_logit)
    qb = rmsnorm(to_heads(br[:, :, 1, 0]), qk_norm_g[2])
    kb = rmsnorm(to_heads(br[:, :, 1, 1]), qk_norm_g[3])
    yb = chunked_relpos_attention(qb, kb, to_heads(br[:, :, 1, 2]), rel_bias)
    yc = stick_breaking_attention(to_heads(br[:, :, 2, 0]), to_heads(br[:, :, 2, 1]),
                                  to_heads(br[:, :, 2, 2]))

    ys = jnp.stack([from_heads(ya), from_heads(yb), from_heads(yc)], axis=2)
    ys = ys * jax.nn.silu(br[:, :, :, 3])
    up = jnp.einsum('bsnw,nwd->bsnd', ys, w_up)
    merged = jnp.sum(gates * up, axis=2)
    return x + jnp.einsum('bsd,de->bse', merged, w_out)


def setup_inputs(seed: int = 0) -> dict:
    key = jax.random.key(seed)
    ks = jax.random.split(key, 10)
    f32 = jnp.float32
    x = jax.random.normal(ks[0], (BATCH, SEQ, D_MODEL), f32)
    norm_g = 1.0 + 0.02 * jax.random.normal(ks[1], (DEPTH, D_MODEL), f32)
    w_in = jax.random.normal(ks[2], (DEPTH, D_MODEL, IN_COLS), f32) * D_MODEL ** -0.5
    b_f = 3.0 + 0.1 * jax.random.normal(ks[3], (DEPTH, N_HEADS), f32)
    b_gate = 0.02 * jax.random.normal(ks[4], (DEPTH, N_BRANCH * D_MODEL), f32)
    qk_norm_g = 1.0 + 0.02 * jax.random.normal(ks[5], (DEPTH, 4, HEAD_DIM), f32)
    rel_bias = 0.5 * jax.random.normal(ks[6], (DEPTH, N_HEADS, N_REL), f32)
    w_up = jax.random.normal(ks[7], (DEPTH, N_BRANCH, BRANCH_W, D_MODEL), f32) * BRANCH_W ** -0.5
    w_out = jax.random.normal(ks[8], (DEPTH, D_MODEL, D_MODEL), f32) * D_MODEL ** -0.5
    return {"x": x, "norm_g": norm_g, "w_in": w_in, "b_f": b_f, "b_gate": b_gate,
            "qk_norm_g": qk_norm_g, "rel_bias": rel_bias, "w_up": w_up, "w_out": w_out}


def reference(x, norm_g, w_in, b_f, b_gate, qk_norm_g, rel_bias, w_up, w_out):
    for layer in range(DEPTH):
        x = hybrid_layer(x, norm_g[layer], w_in[layer], b_f[layer], b_gate[layer],
                         qk_norm_g[layer], rel_bias[layer], w_up[layer], w_out[layer])
    return x
```

```python
import functools
import math

import jax
import jax.numpy as jnp
from jax import lax
from jax.experimental import pallas as pl
from jax.experimental.pallas import tpu as pltpu

N_HEADS = 4
HEAD_DIM = 128
BRANCH_W = N_HEADS * HEAD_DIM
N_BRANCH = 3
CHUNK = 64
LEFT_CHUNKS = 8
BAND = (LEFT_CHUNKS + 1) * CHUNK
MAX_REL = 128
EPS = 1e-6
NEG = -1e30
LOG2E = math.log2(math.e)

LANES = 128
VMEM_LIMIT = 56 * 1024 * 1024

PROJ_TM = 512
ATT_TQ = 512
FOX_TK = 512
SB_TK = 256
TAIL_TM = 512
TAIL_TN = 256

f32 = jnp.float32
bf16 = jnp.bfloat16


def _cparams(sem):
    return pltpu.CompilerParams(dimension_semantics=sem, vmem_limit_bytes=VMEM_LIMIT)


def _dot_nt(a, b):
    return lax.dot_general(a, b, (((1,), (1,)), ((), ())), preferred_element_type=f32)


def _qkv_kernel(x_ref, g_ref, w_ref, wf_ref, gain_ref, qkv_ref, hn_ref, ft_ref, hn_s):
    j = pl.program_id(1)

    @pl.when(j == 0)
    def _():
        x = x_ref[...]
        ms = jnp.mean(x * x, axis=-1, keepdims=True)
        hb = (x * lax.rsqrt(ms + EPS) * g_ref[...]).astype(bf16)
        hn_s[...] = hb
        hn_ref[...] = hb
        ft_ref[...] = _dot_nt(wf_ref[...], hb)

    acc = jnp.dot(hn_s[...], w_ref[...], preferred_element_type=f32)
    gain = gain_ref[0]

    @pl.when(j < 2)
    def _():
        for c0 in range(0, 2 * BRANCH_W, HEAD_DIM):
            a = acc[:, c0:c0 + HEAD_DIM]
            rs = lax.rsqrt(jnp.mean(a * a, axis=-1, keepdims=True) + EPS)
            qkv_ref[:, c0:c0 + HEAD_DIM] = (a * rs * gain[:, c0:c0 + HEAD_DIM]).astype(bf16)
        qkv_ref[:, 2 * BRANCH_W:] = acc[:, 2 * BRANCH_W:].astype(bf16)

    @pl.when(j == 2)
    def _():
        qkv_ref[...] = (acc * gain).astype(bf16)


def _qkv_proj(x2, norm_g, w_qkv, wf_t, gains):
    m, d = x2.shape
    tm = PROJ_TM
    tn = 3 * BRANCH_W
    return pl.pallas_call(
        _qkv_kernel,
        grid=(m // tm, N_BRANCH),
        in_specs=[
            pl.BlockSpec((tm, d), lambda i, j: (i, 0)),
            pl.BlockSpec((1, d), lambda i, j: (0, 0)),
            pl.BlockSpec((d, tn), lambda i, j: (0, j)),
            pl.BlockSpec((8, d), lambda i, j: (0, 0)),
            pl.BlockSpec((1, 1, tn), lambda i, j: (j, 0, 0)),
        ],
        out_specs=[
            pl.BlockSpec((tm, tn), lambda i, j: (i, j)),
            pl.BlockSpec((tm, d), lambda i, j: (i, 0)),
            pl.BlockSpec((8, tm), lambda i, j: (0, i)),
        ],
        out_shape=[
            jax.ShapeDtypeStruct((m, N_BRANCH * tn), bf16),
            jax.ShapeDtypeStruct((m, d), bf16),
            jax.ShapeDtypeStruct((8, m), f32),
        ],
        scratch_shapes=[pltpu.VMEM((tm, d), bf16)],
        compiler_params=_cparams(("parallel", "arbitrary")),
        name="qkv_proj",
    )(x2, norm_g, w_qkv, wf_t, gains)


def _gate_kernel(hn_ref, w_ref, o_ref):
    a = jnp.dot(hn_ref[...], w_ref[...], preferred_element_type=f32)
    o_ref[...] = a / (1.0 + jnp.exp(-a))


def _gate_proj(hn, w_gate):
    m, d = hn.shape
    n = w_gate.shape[1]
    tm = PROJ_TM
    return pl.pallas_call(
        _gate_kernel,
        grid=(m // tm,),
        in_specs=[
            pl.BlockSpec((tm, d), lambda i: (i, 0)),
            pl.BlockSpec((d, n), lambda i: (0, 0)),
        ],
        out_specs=pl.BlockSpec((tm, n), lambda i: (i, 0)),
        out_shape=jax.ShapeDtypeStruct((m, n), f32),
        compiler_params=_cparams(("parallel",)),
        name="gate_proj",
    )(hn, w_gate)


def _cumsum_kernel(bf_ref, f_ref, c_ref):
    h = pl.program_id(0)
    x = f_ref[0, 0] + bf_ref[h]
    lf = jnp.minimum(x, 0.0) - jnp.log(1.0 + jnp.exp(-jnp.abs(x)))
    r = lf.shape[0]
    hp = lax.Precision.HIGHEST
    row = lax.broadcasted_iota(jnp.int32, (LANES, LANES), 0)
    col = lax.broadcasted_iota(jnp.int32, (LANES, LANES), 1)
    upper = (row <= col).astype(f32)
    within = jnp.dot(lf, upper, precision=hp, preferred_element_type=f32)
    tot = jnp.broadcast_to(within[:, LANES - 1:LANES], (r, LANES))
    rr = lax.broadcasted_iota(jnp.int32, (r, r), 0)
    rc = lax.broadcasted_iota(jnp.int32, (r, r), 1)
    earlier = (rc < rr).astype(f32)
    offs = jnp.dot(earlier, tot, precision=hp, preferred_element_type=f32)
    c_ref[0, 0] = (within + offs) * LOG2E


def _forget_cumsum(f_t, b_f, batch, seq):
    r = seq // LANES
    f4 = f_t[:N_HEADS].reshape(N_HEADS, batch, r, LANES)
    c = pl.pallas_call(
        _cumsum_kernel,
        grid=(N_HEADS, batch),
        in_specs=[
            pl.BlockSpec(memory_space=pltpu.SMEM),
            pl.BlockSpec((1, 1, r, LANES), lambda h, b: (h, b, 0, 0)),
        ],
        out_specs=pl.BlockSpec((1, 1, r, LANES), lambda h, b: (h, b, 0, 0)),
        out_shape=jax.ShapeDtypeStruct((N_HEADS, batch, r, LANES), f32),
        compiler_params=_cparams(("parallel", "parallel")),
        name="forget_cumsum",
    )(b_f, f4)
    return c.reshape(N_HEADS, batch, seq // FOX_TK, FOX_TK)


def _fox_kernel(q_ref, k_ref, v_ref, c_ref, g_ref, o_ref, m_s, l_s, acc_s):
    i = pl.program_id(2)
    tq, tk = ATT_TQ, FOX_TK
    q = q_ref[0]

    m_s[...] = jnp.full(m_s.shape, NEG, f32)
    l_s[...] = jnp.zeros(l_s.shape, f32)
    acc_s[...] = jnp.zeros(acc_s.shape, f32)

    def step(j, masked):
        off = pl.multiple_of(j * tk, tk)
        k = k_ref[0, pl.ds(off, tk), :]
        v = v_ref[0, pl.ds(off, tk), :]
        s = _dot_nt(q, k) - c_ref[0, 0, pl.ds(j, 1), :]
        if masked:
            row = lax.broadcasted_iota(jnp.int32, (tq, tk), 0)
            col = lax.broadcasted_iota(jnp.int32, (tq, tk), 1)
            s = jnp.where(col <= row, s, NEG)
        m_old = m_s[...]
        m_new = jnp.maximum(m_old, jnp.max(s, axis=-1, keepdims=True))
        alpha = jnp.exp2(m_old - m_new)
        p = jnp.exp2(s - m_new)
        l_s[...] = alpha * l_s[...] + jnp.sum(p, axis=-1, keepdims=True)
        acc_s[...] = alpha * acc_s[...] + jnp.dot(p.astype(bf16), v, preferred_element_type=f32)
        m_s[...] = m_new

    def body(j, carry):
        step(j, False)
        return carry

    lax.fori_loop(0, i, body, 0)
    step(i, True)
    o_ref[0] = (acc_s[...] / l_s[...] * g_ref[0]).astype(bf16)


def _fox_attention(qkv, c, gsilu, batch, seq):
    tq = ATT_TQ
    assert FOX_TK == tq
    qkv3 = qkv.reshape(batch, seq, -1)
    g3 = gsilu.reshape(batch, seq, -1)
    return pl.pallas_call(
        _fox_kernel,
        grid=(batch, N_HEADS, seq // tq),
        in_specs=[
            pl.BlockSpec((1, tq, HEAD_DIM), lambda b, h, i: (b, i, h)),
            pl.BlockSpec((1, seq, HEAD_DIM), lambda b, h, i: (b, 0, N_HEADS + h)),
            pl.BlockSpec((1, seq, HEAD_DIM), lambda b, h, i: (b, 0, 2 * N_HEADS + h)),
            pl.BlockSpec((1, 1, seq // FOX_TK, FOX_TK), lambda b, h, i: (h, b, 0, 0)),
            pl.BlockSpec((1, tq, HEAD_DIM), lambda b, h, i: (b, i, h)),
        ],
        out_specs=pl.BlockSpec((1, tq, HEAD_DIM), lambda b, h, i: (b, i, h)),
        out_shape=jax.ShapeDtypeStruct((batch, seq, BRANCH_W), bf16),
        scratch_shapes=[
            pltpu.VMEM((tq, 1), f32),
            pltpu.VMEM((tq, 1), f32),
            pltpu.VMEM((tq, HEAD_DIM), f32),
        ],
        compiler_params=_cparams(("parallel", "parallel", "arbitrary")),
        name="fox_attention",
    )(qkv3, qkv3, qkv3, c, g3)


def _chunk_kernel(q_ref, kp_ref, kc_ref, vp_ref, vc_ref, bias_ref, g_ref, o_ref, k_s, v_s):
    i = pl.program_id(2)
    span = LEFT_CHUNKS * CHUNK
    k_s[0:span] = kp_ref[0]
    k_s[span:2 * span] = kc_ref[0]
    v_s[0:span] = vp_ref[0]
    v_s[span:2 * span] = vc_ref[0]
    bias = bias_ref[0]
    col = lax.broadcasted_iota(jnp.int32, (CHUNK, BAND), 1)
    for c in range(LEFT_CHUNKS):
        qc = q_ref[0, c * CHUNK:(c + 1) * CHUNK, :]
        kb = k_s[c * CHUNK:c * CHUNK + BAND]
        vb = v_s[c * CHUNK:c * CHUNK + BAND]
        s = _dot_nt(qc, kb) + bias
        valid = jnp.logical_or(i > 0, col >= span - c * CHUNK)
        s = jnp.where(valid, s, NEG)
        m = jnp.max(s, axis=-1, keepdims=True)
        p = jnp.exp2(s - m)
        l = jnp.sum(p, axis=-1, keepdims=True)
        o = jnp.dot(p.astype(bf16), vb, preferred_element_type=f32)
        o_ref[0, c * CHUNK:(c + 1) * CHUNK, :] = (
            o / l * g_ref[0, c * CHUNK:(c + 1) * CHUNK, :]).astype(bf16)


def _chunk_attention(qkv, bias2, gsilu, batch, seq):
    span = LEFT_CHUNKS * CHUNK
    qkv3 = qkv.reshape(batch, seq, -1)
    g3 = gsilu.reshape(batch, seq, -1)
    base = 3 * N_HEADS

    def prev(b, h, i, seg):
        return (b, jnp.maximum(i - 1, 0), base + seg * N_HEADS + h)

    return pl.pallas_call(
        _chunk_kernel,
        grid=(batch, N_HEADS, seq // span),
        in_specs=[
            pl.BlockSpec((1, span, HEAD_DIM), lambda b, h, i: (b, i, base + h)),
            pl.BlockSpec((1, span, HEAD_DIM), lambda b, h, i: prev(b, h, i, 1)),
            pl.BlockSpec((1, span, HEAD_DIM), lambda b, h, i: (b, i, base + N_HEADS + h)),
            pl.BlockSpec((1, span, HEAD_DIM), lambda b, h, i: prev(b, h, i, 2)),
            pl.BlockSpec((1, span, HEAD_DIM), lambda b, h, i: (b, i, base + 2 * N_HEADS + h)),
            pl.BlockSpec((1, CHUNK, BAND), lambda b, h, i: (h, 0, 0)),
            pl.BlockSpec((1, span, HEAD_DIM), lambda b, h, i: (b, i, N_HEADS + h)),
        ],
        out_specs=pl.BlockSpec((1, span, HEAD_DIM), lambda b, h, i: (b, i, h)),
        out_shape=jax.ShapeDtypeStruct((batch, seq, BRANCH_W), bf16),
        scratch_shapes=[
            pltpu.VMEM((2 * span, HEAD_DIM), bf16),
            pltpu.VMEM((2 * span, HEAD_DIM), bf16),
        ],
        compiler_params=_cparams(("parallel", "parallel", "arbitrary")),
        name="chunk_attention",
    )(qkv3, qkv3, qkv3, qkv3, qkv3, bias2, g3)


def _stick_kernel(q_ref, k_ref, v_ref, g_ref, o_ref, carry_s, acc_s):
    i = pl.program_id(2)
    tq, tk = ATT_TQ, SB_TK
    per = tq // tk
    q = q_ref[0]
    carry_s[...] = jnp.zeros(carry_s.shape, f32)
    acc_s[...] = jnp.zeros(acc_s.shape, f32)

    kr = lax.broadcasted_iota(jnp.int32, (2 * tk, tk), 0)
    kc = lax.broadcasted_iota(jnp.int32, (2 * tk, tk), 1)
    kr = jnp.where(kr >= tk, kr - tk, kr)
    suffix = (kr >= kc).astype(bf16)

    def step(j, masked):
        off = pl.multiple_of(j * tk, tk)
        k = k_ref[0, pl.ds(off, tk), :]
        v = v_ref[0, pl.ds(off, tk), :]
        z = _dot_nt(q, k)
        p = jnp.maximum(z, 0.0) + jnp.log2(1.0 + jnp.exp2(-jnp.abs(z)))
        if masked:
            row = i * tq + lax.broadcasted_iota(jnp.int32, (tq, tk), 0)
            col = j * tk + lax.broadcasted_iota(jnp.int32, (tq, tk), 1)
            keep = col < row
            p = jnp.where(keep, p, 0.0)
        hi = p.astype(bf16)
        lo = (p - hi.astype(f32)).astype(bf16)
        rp = jnp.dot(jnp.concatenate([hi, lo], axis=1), suffix, preferred_element_type=f32)
        carry = carry_s[...]
        a = jnp.exp2(z - rp - carry)
        if masked:
            a = jnp.where(keep, a, 0.0)
        acc_s[...] += jnp.dot(a.astype(bf16), v, preferred_element_type=f32)
        carry_s[...] = carry + rp[:, 0:1]

    last = i * per + per - 1
    for d in range(per):
        step(last - d, True)

    def body(n, c):
        step(i * per - 1 - n, False)
        return c

    lax.fori_loop(0, i * per, body, 0)
    o_ref[0] = (acc_s[...] * g_ref[0]).astype(bf16)


def _stick_attention(qkv, gsilu, batch, seq):
    tq = ATT_TQ
    qkv3 = qkv.reshape(batch, seq, -1)
    g3 = gsilu.reshape(batch, seq, -1)
    base = 6 * N_HEADS
    return pl.pallas_call(
        _stick_kernel,
        grid=(batch, N_HEADS, seq // tq),
        in_specs=[
            pl.BlockSpec((1, tq, HEAD_DIM), lambda b, h, i: (b, i, base + h)),
            pl.BlockSpec((1, seq, HEAD_DIM), lambda b, h, i: (b, 0, base + N_HEADS + h)),
            pl.BlockSpec((1, seq, HEAD_DIM), lambda b, h, i: (b, 0, base + 2 * N_HEADS + h)),
            pl.BlockSpec((1, tq, HEAD_DIM), lambda b, h, i: (b, i, 2 * N_HEADS + h)),
        ],
        out_specs=pl.BlockSpec((1, tq, HEAD_DIM), lambda b, h, i: (b, i, h)),
        out_shape=jax.ShapeDtypeStruct((batch, seq, BRANCH_W), bf16),
        scratch_shapes=[
            pltpu.VMEM((tq, 1), f32),
            pltpu.VMEM((tq, HEAD_DIM), f32),
        ],
        compiler_params=_cparams(("parallel", "parallel", "arbitrary")),
        name="stick_attention",
    )(qkv3, qkv3, qkv3, g3)


def _tail_kernel(x_ref, hn_ref, ya_ref, yb_ref, yc_ref, wg0_ref, wg1_ref, wg2_ref,
                 bg0_ref, bg1_ref, bg2_ref, wup_ref, wout_ref, o_ref):
    j = pl.program_id(1)

    @pl.when(j == 0)
    def _():
        o_ref[...] = x_ref[...]

    hn = hn_ref[...]
    merged = None
    branches = ((ya_ref, wg0_ref, bg0_ref), (yb_ref, wg1_ref, bg1_ref), (yc_ref, wg2_ref, bg2_ref))
    for n, (y_ref, wg_ref, bg_ref) in enumerate(branches):
        gl = jnp.dot(hn, wg_ref[...], preferred_element_type=f32) + bg_ref[...]
        gate = 1.0 / (1.0 + jnp.exp(-gl))
        up = jnp.dot(y_ref[...], wup_ref[n], preferred_element_type=f32)
        merged = gate * up if merged is None else merged + gate * up
    o_ref[...] += jnp.dot(merged.astype(bf16), wout_ref[...], preferred_element_type=f32)


def _tail(x2, hn, ya, yb, yc, w_mg, b_mg, w_up, w_out):
    m, d = x2.shape
    tm, tn = TAIL_TM, TAIL_TN
    nj = d // tn

    def wg_spec(n):
        return pl.BlockSpec((d, tn), lambda i, j: (0, n * nj + j))

    def bg_spec(n):
        return pl.BlockSpec((1, tn), lambda i, j: (0, n * nj + j))

    y_spec = pl.BlockSpec((tm, BRANCH_W), lambda i, j: (i, 0))
    return pl.pallas_call(
        _tail_kernel,
        grid=(m // tm, nj),
        in_specs=[
            pl.BlockSpec((tm, d), lambda i, j: (i, 0)),
            pl.BlockSpec((tm, d), lambda i, j: (i, 0)),
            y_spec, y_spec, y_spec,
            wg_spec(0), wg_spec(1), wg_spec(2),
            bg_spec(0), bg_spec(1), bg_spec(2),
            pl.BlockSpec((N_BRANCH, BRANCH_W, tn), lambda i, j: (0, 0, j)),
            pl.BlockSpec((tn, d), lambda i, j: (j, 0)),
        ],
        out_specs=pl.BlockSpec((tm, d), lambda i, j: (i, 0)),
        out_shape=jax.ShapeDtypeStruct((m, d), f32),
        compiler_params=_cparams(("parallel", "arbitrary")),
        name="merge_tail",
    )(x2, hn, ya, yb, yc, w_mg, w_mg, w_mg, b_mg, b_mg, b_mg, w_up, w_out)


def _layer(x2, batch, seq, norm_g, w_in, b_f, b_gate, qk_norm_g, rel_bias, w_up, w_out):
    d = x2.shape[1]
    qkvg = N_BRANCH * 4 * BRANCH_W
    scale2 = HEAD_DIM ** -0.5 * LOG2E

    wb = w_in[:, :qkvg].reshape(d, N_BRANCH, 4, BRANCH_W)
    w_qkv = wb[:, :, :3].reshape(d, N_BRANCH * 3 * BRANCH_W).astype(bf16)
    w_gate = wb[:, :, 3].reshape(d, N_BRANCH * BRANCH_W).astype(bf16)
    wf_t = jnp.zeros((8, d), f32).at[:N_HEADS].set(w_in[:, qkvg:qkvg + N_HEADS].T).astype(bf16)
    w_mg = w_in[:, qkvg + N_HEADS:].astype(bf16)
    b_mg = b_gate.reshape(1, N_BRANCH * d)

    ones = jnp.ones((BRANCH_W,), f32)
    gq = lambda g: jnp.tile(g, N_HEADS)
    gains = jnp.stack([
        jnp.concatenate([gq(qk_norm_g[0]) * scale2, gq(qk_norm_g[1]), ones]),
        jnp.concatenate([gq(qk_norm_g[2]) * scale2, gq(qk_norm_g[3]), ones]),
        jnp.concatenate([ones * scale2, ones, ones]),
    ]).reshape(N_BRANCH, 1, 3 * BRANCH_W)

    pad = LEFT_CHUNKS * CHUNK
    rel = pad + jnp.arange(CHUNK)[:, None] - jnp.arange(BAND)[None, :]
    idx = jnp.clip(rel, -MAX_REL, MAX_REL) + MAX_REL
    bias2 = rel_bias[:, idx] * LOG2E

    qkv, hn, f_t = _qkv_proj(x2, norm_g.reshape(1, d), w_qkv, wf_t, gains)
    gsilu = _gate_proj(hn, w_gate)
    c = _forget_cumsum(f_t, b_f, batch, seq)
    ya = _fox_attention(qkv, c, gsilu, batch, seq).reshape(batch * seq, BRANCH_W)
    yb = _chunk_attention(qkv, bias2, gsilu, batch, seq).reshape(batch * seq, BRANCH_W)
    yc = _stick_attention(qkv, gsilu, batch, seq).reshape(batch * seq, BRANCH_W)
    return _tail(x2, hn, ya, yb, yc, w_mg, b_mg, w_up.astype(bf16), w_out.astype(bf16))


def kernel(x, norm_g, w_in, b_f, b_gate, qk_norm_g, rel_bias, w_up, w_out):
    batch, seq, d = x.shape
    x2 = x.reshape(batch * seq, d)
    for layer in range(norm_g.shape[0]):
        x2 = _layer(x2, batch, seq, norm_g[layer], w_in[layer], b_f[layer], b_gate[layer],
                    qk_norm_g[layer], rel_bias[layer], w_up[layer], w_out[layer])
    return x2.reshape(batch, seq, d)
```

```python
import math

import jax
import jax.numpy as jnp
from jax import lax
from jax.experimental import pallas as pl
from jax.experimental.pallas import tpu as pltpu

N_HEADS = 4
HEAD_DIM = 128
BRANCH_W = N_HEADS * HEAD_DIM
N_BRANCH = 3
CHUNK = 64
LEFT_CHUNKS = 8
BAND = (LEFT_CHUNKS + 1) * CHUNK
MAX_REL = 128
EPS = 1e-6
NEG = -1e30
LOG2E = math.log2(math.e)

LANES = 128
VMEM_LIMIT = 56 * 1024 * 1024

PROJ_TM = 512
ATT_TQ = 512
FOX_TK = 512
SB_TK = 256
TAIL_TM = 512
TAIL_TN = 256

UNDERFLOW_LOG2 = 152.0
BF16_ROUND = 1.0 + 2.0 ** -8

f32 = jnp.float32
bf16 = jnp.bfloat16


def _cparams(sem):
    return pltpu.CompilerParams(dimension_semantics=sem, vmem_limit_bytes=VMEM_LIMIT)


def _dot_nt(a, b):
    return lax.dot_general(a, b, (((1,), (1,)), ((), ())), preferred_element_type=f32)


def _lane_tile(a, n):
    return a if n == 1 else jnp.concatenate([a] * n, axis=1)


def _proj_kernel(x_ref, g_ref, wq_ref, wk_ref, wv_ref, wg_ref, wf_ref, gain_ref,
                 qkv_ref, gs_ref, hn_ref, ft_ref, hn_s):
    j = pl.program_id(1)

    @pl.when(j == 0)
    def _():
        x = x_ref[...]
        ms = jnp.mean(x * x, axis=-1, keepdims=True)
        hb = (x * lax.rsqrt(ms + EPS) * g_ref[...]).astype(bf16)
        hn_s[...] = hb
        hn_ref[...] = hb
        ft_ref[...] = _dot_nt(wf_ref[...], hb)

    hn = hn_s[...]
    gain = gain_ref[...]

    def normed(acc, gcols):
        parts = []
        for c0 in range(0, BRANCH_W, HEAD_DIM):
            a = acc[:, c0:c0 + HEAD_DIM]
            rs = lax.rsqrt(jnp.mean(a * a, axis=-1, keepdims=True) + EPS)
            parts.append((a * rs * gcols[:, c0:c0 + HEAD_DIM]).astype(bf16))
        return jnp.concatenate(parts, axis=1)

    for part, w_ref in enumerate((wq_ref, wk_ref)):
        acc = jnp.dot(hn, w_ref[...], preferred_element_type=f32)
        gcols = gain[:, part * BRANCH_W:(part + 1) * BRANCH_W]
        sl = slice(part * BRANCH_W, (part + 1) * BRANCH_W)

        @pl.when(j < 2)
        def _():
            qkv_ref[:, sl] = normed(acc, gcols)

        @pl.when(j == 2)
        def _():
            qkv_ref[:, sl] = (acc * gcols).astype(bf16)

    qkv_ref[:, 2 * BRANCH_W:] = jnp.dot(hn, wv_ref[...], preferred_element_type=f32).astype(bf16)
    a = jnp.dot(hn, wg_ref[...], preferred_element_type=f32)
    gs_ref[...] = a / (1.0 + jnp.exp(-a))


def _proj(x2, norm_g3, w16, wf_t, gains, layer):
    m, d = x2.shape
    tm = PROJ_TM

    def w_spec(part):
        return pl.BlockSpec((None, d, BRANCH_W), lambda i, j: (layer, 0, 4 * j + part))

    return pl.pallas_call(
        _proj_kernel,
        grid=(m // tm, N_BRANCH),
        in_specs=[
            pl.BlockSpec((tm, d), lambda i, j: (i, 0)),
            pl.BlockSpec((None, 1, d), lambda i, j: (layer, 0, 0)),
            w_spec(0), w_spec(1), w_spec(2), w_spec(3),
            pl.BlockSpec((None, 8, d), lambda i, j: (layer, 0, 0)),
            pl.BlockSpec((None, 1, 3 * BRANCH_W), lambda i, j: (j, 0, 0)),
        ],
        out_specs=[
            pl.BlockSpec((tm, 3 * BRANCH_W), lambda i, j: (i, j)),
            pl.BlockSpec((tm, BRANCH_W), lambda i, j: (i, j)),
            pl.BlockSpec((tm, d), lambda i, j: (i, 0)),
            pl.BlockSpec((8, tm), lambda i, j: (0, i)),
        ],
        out_shape=[
            jax.ShapeDtypeStruct((m, N_BRANCH * 3 * BRANCH_W), bf16),
            jax.ShapeDtypeStruct((m, N_BRANCH * BRANCH_W), f32),
            jax.ShapeDtypeStruct((m, d), bf16),
            jax.ShapeDtypeStruct((8, m), f32),
        ],
        scratch_shapes=[pltpu.VMEM((tm, d), bf16)],
        compiler_params=_cparams(("parallel", "arbitrary")),
        name="proj",
    )(x2, norm_g3, w16, w16, w16, w16, wf_t, gains)


def _cumsum_kernel(bf_ref, f_ref, c_ref):
    h = pl.program_id(0)
    x = f_ref[0, 0] + bf_ref[h]
    lf = jnp.minimum(x, 0.0) - jnp.log(1.0 + jnp.exp(-jnp.abs(x)))
    r = lf.shape[0]
    hp = lax.Precision.HIGHEST
    row = lax.broadcasted_iota(jnp.int32, (LANES, LANES), 0)
    col = lax.broadcasted_iota(jnp.int32, (LANES, LANES), 1)
    upper = (row <= col).astype(f32)
    within = jnp.dot(lf, upper, precision=hp, preferred_element_type=f32)
    tot = jnp.broadcast_to(within[:, LANES - 1:LANES], (r, LANES))
    rr = lax.broadcasted_iota(jnp.int32, (r, r), 0)
    rc = lax.broadcasted_iota(jnp.int32, (r, r), 1)
    earlier = (rc < rr).astype(f32)
    offs = jnp.dot(earlier, tot, precision=hp, preferred_element_type=f32)
    c_ref[0, 0] = (within + offs) * LOG2E


def _forget_cumsum(f_t, b_f, batch, seq):
    r = seq // LANES
    f4 = f_t[:N_HEADS].reshape(N_HEADS, batch, r, LANES)
    c = pl.pallas_call(
        _cumsum_kernel,
        grid=(N_HEADS, batch),
        in_specs=[
            pl.BlockSpec(memory_space=pltpu.SMEM),
            pl.BlockSpec((1, 1, r, LANES), lambda h, b: (h, b, 0, 0)),
        ],
        out_specs=pl.BlockSpec((1, 1, r, LANES), lambda h, b: (h, b, 0, 0)),
        out_shape=jax.ShapeDtypeStruct((N_HEADS, batch, r, LANES), f32),
        compiler_params=_cparams(("parallel", "parallel")),
        name="forget_cumsum",
    )(b_f, f4)
    return c.reshape(N_HEADS, batch, seq // FOX_TK, FOX_TK)


def _fox_kernel(zb_ref, cb_ref, q_ref, k_ref, v_ref, c_ref, g_ref, o_ref, m_s, l_s, acc_s):
    b = pl.program_id(0)
    h = pl.program_id(1)
    i = pl.program_id(2)
    tq, tk = ATT_TQ, FOX_TK
    nt = pl.num_programs(2)
    rep = tk // LANES
    q = q_ref[0]

    m_s[...] = jnp.full(m_s.shape, NEG, f32)
    l_s[...] = jnp.zeros(l_s.shape, f32)
    acc_s[...] = jnp.zeros(acc_s.shape, f32)

    def step(j, masked):
        off = pl.multiple_of(j * tk, tk)
        k = k_ref[0, pl.ds(off, tk), :]
        v = v_ref[0, pl.ds(off, tk), :]
        s = _dot_nt(q, k) - c_ref[0, 0, pl.ds(j, 1), :]
        if masked:
            row = lax.broadcasted_iota(jnp.int32, (tq, tk), 0)
            col = lax.broadcasted_iota(jnp.int32, (tq, tk), 1)
            s = jnp.where(col <= row, s, NEG)
        m_old = m_s[...]
        m_new = jnp.maximum(m_old, jnp.max(s, axis=-1, keepdims=True))
        alpha = jnp.exp2(m_old - m_new)
        p = jnp.exp2(s - _lane_tile(m_new, rep))
        lsum = p[:, 0:LANES]
        for cblk in range(1, rep):
            lsum = lsum + p[:, cblk * LANES:(cblk + 1) * LANES]
        l_s[...] = alpha * l_s[...] + lsum
        acc_s[...] = alpha * acc_s[...] + jnp.dot(p.astype(bf16), v, preferred_element_type=f32)
        m_s[...] = m_new

    step(i, True)

    thr = jnp.min(m_s[...]) - zb_ref[0] - UNDERFLOW_LOG2
    base = (h * pl.num_programs(0) + b) * nt

    def cond(j):
        return jnp.logical_and(j >= 0, cb_ref[base + jnp.maximum(j, 0)] > thr)

    def body(j):
        step(j, False)
        return j - 1

    lax.while_loop(cond, body, i - 1)
    l = jnp.sum(l_s[...], axis=-1, keepdims=True)
    o_ref[0] = (acc_s[...] / l * g_ref[0]).astype(bf16)


def _fox_attention(qkv3, c, cb, zb, gs3):
    batch, seq, _ = qkv3.shape
    tq = ATT_TQ
    assert FOX_TK == tq
    return pl.pallas_call(
        _fox_kernel,
        grid=(batch, N_HEADS, seq // tq),
        in_specs=[
            pl.BlockSpec(memory_space=pltpu.SMEM),
            pl.BlockSpec(memory_space=pltpu.SMEM),
            pl.BlockSpec((1, tq, HEAD_DIM), lambda b, h, i: (b, i, h)),
            pl.BlockSpec((1, seq, HEAD_DIM), lambda b, h, i: (b, 0, N_HEADS + h)),
            pl.BlockSpec((1, seq, HEAD_DIM), lambda b, h, i: (b, 0, 2 * N_HEADS + h)),
            pl.BlockSpec((1, 1, seq // FOX_TK, FOX_TK), lambda b, h, i: (h, b, 0, 0)),
            pl.BlockSpec((1, tq, HEAD_DIM), lambda b, h, i: (b, i, h)),
        ],
        out_specs=pl.BlockSpec((1, tq, HEAD_DIM), lambda b, h, i: (b, i, h)),
        out_shape=jax.ShapeDtypeStruct((batch, seq, BRANCH_W), bf16),
        scratch_shapes=[
            pltpu.VMEM((tq, LANES), f32),
            pltpu.VMEM((tq, LANES), f32),
            pltpu.VMEM((tq, HEAD_DIM), f32),
        ],
        compiler_params=_cparams(("parallel", "parallel", "arbitrary")),
        name="fox_attention",
    )(zb, cb, qkv3, qkv3, qkv3, c, gs3)


def _chunk_kernel(q_ref, kp_ref, kc_ref, vp_ref, vc_ref, bias_ref, g_ref, o_ref, k_s, v_s):
    i = pl.program_id(2)
    span = LEFT_CHUNKS * CHUNK
    k_s[0:span] = kp_ref[0]
    k_s[span:2 * span] = kc_ref[0]
    v_s[0:span] = vp_ref[0]
    v_s[span:2 * span] = vc_ref[0]
    bias = bias_ref[0]
    col = lax.broadcasted_iota(jnp.int32, (CHUNK, BAND), 1)
    for c in range(LEFT_CHUNKS):
        qc = q_ref[0, c * CHUNK:(c + 1) * CHUNK, :]
        kb = k_s[c * CHUNK:c * CHUNK + BAND]
        vb = v_s[c * CHUNK:c * CHUNK + BAND]
        s = _dot_nt(qc, kb) + bias
        valid = jnp.logical_or(i > 0, col >= span - c * CHUNK)
        s = jnp.where(valid, s, NEG)
        m = jnp.max(s, axis=-1, keepdims=True)
        p = jnp.exp2(s - m)
        l = jnp.sum(p, axis=-1, keepdims=True)
        o = jnp.dot(p.astype(bf16), vb, preferred_element_type=f32)
        o_ref[0, c * CHUNK:(c + 1) * CHUNK, :] = (
            o / l * g_ref[0, c * CHUNK:(c + 1) * CHUNK, :]).astype(bf16)


def _chunk_attention(qkv3, bias2, gs3):
    batch, seq, _ = qkv3.shape
    span = LEFT_CHUNKS * CHUNK
    base = 3 * N_HEADS

    def prev(b, h, i, seg):
        return (b, jnp.maximum(i - 1, 0), base + seg * N_HEADS + h)

    return pl.pallas_call(
        _chunk_kernel,
        grid=(batch, N_HEADS, seq // span),
        in_specs=[
            pl.BlockSpec((1, span, HEAD_DIM), lambda b, h, i: (b, i, base + h)),
            pl.BlockSpec((1, span, HEAD_DIM), lambda b, h, i: prev(b, h, i, 1)),
            pl.BlockSpec((1, span, HEAD_DIM), lambda b, h, i: (b, i, base + N_HEADS + h)),
            pl.BlockSpec((1, span, HEAD_DIM), lambda b, h, i: prev(b, h, i, 2)),
            pl.BlockSpec((1, span, HEAD_DIM), lambda b, h, i: (b, i, base + 2 * N_HEADS + h)),
            pl.BlockSpec((1, CHUNK, BAND), lambda b, h, i: (h, 0, 0)),
            pl.BlockSpec((1, span, HEAD_DIM), lambda b, h, i: (b, i, N_HEADS + h)),
        ],
        out_specs=pl.BlockSpec((1, span, HEAD_DIM), lambda b, h, i: (b, i, h)),
        out_shape=jax.ShapeDtypeStruct((batch, seq, BRANCH_W), bf16),
        scratch_shapes=[
            pltpu.VMEM((2 * span, HEAD_DIM), bf16),
            pltpu.VMEM((2 * span, HEAD_DIM), bf16),
        ],
        compiler_params=_cparams(("parallel", "parallel", "arbitrary")),
        name="chunk_attention",
    )(qkv3, qkv3, qkv3, qkv3, qkv3, bias2, gs3)


def _stick_kernel(q_ref, k_ref, v_ref, g_ref, o_ref, carry_s, acc_s):
    i = pl.program_id(2)
    tq, tk = ATT_TQ, SB_TK
    per = tq // tk
    rep = tk // LANES
    q = q_ref[0]
    carry_s[...] = jnp.zeros(carry_s.shape, f32)
    acc_s[...] = jnp.zeros(acc_s.shape, f32)

    kr = lax.broadcasted_iota(jnp.int32, (2 * tk, tk), 0)
    kc = lax.broadcasted_iota(jnp.int32, (2 * tk, tk), 1)
    kr = jnp.where(kr >= tk, kr - tk, kr)
    suffix = (kr >= kc).astype(bf16)

    def step(j, masked):
        off = pl.multiple_of(j * tk, tk)
        k = k_ref[0, pl.ds(off, tk), :]
        v = v_ref[0, pl.ds(off, tk), :]
        z = _dot_nt(q, k)
        p = jnp.maximum(z, 0.0) + jnp.log2(1.0 + jnp.exp2(-jnp.abs(z)))
        if masked:
            row = i * tq + lax.broadcasted_iota(jnp.int32, (tq, tk), 0)
            col = j * tk + lax.broadcasted_iota(jnp.int32, (tq, tk), 1)
            keep = col < row
            p = jnp.where(keep, p, 0.0)
        hi = p.astype(bf16)
        lo = (p - hi.astype(f32)).astype(bf16)
        rp = jnp.dot(jnp.concatenate([hi, lo], axis=1), suffix, preferred_element_type=f32)
        carry = carry_s[...]
        a = jnp.exp2(z - rp - _lane_tile(carry, rep))
        if masked:
            a = jnp.where(keep, a, 0.0)
        acc_s[...] += jnp.dot(a.astype(bf16), v, preferred_element_type=f32)
        carry_s[...] = carry + jnp.broadcast_to(rp[:, 0:1], (tq, LANES))

    last = i * per + per - 1
    for d in range(per):
        step(last - d, True)

    def cond(state):
        j, cmin = state
        return jnp.logical_and(j >= 0, cmin < UNDERFLOW_LOG2)

    def body(state):
        j, _ = state
        step(j, False)
        return j - 1, jnp.min(carry_s[...])

    lax.while_loop(cond, body, (i * per - 1, jnp.min(carry_s[...])))
    o_ref[0] = (acc_s[...] * g_ref[0]).astype(bf16)


def _stick_attention(qkv3, gs3):
    batch, seq, _ = qkv3.shape
    tq = ATT_TQ
    base = 6 * N_HEADS
    return pl.pallas_call(
        _stick_kernel,
        grid=(batch, N_HEADS, seq // tq),
        in_specs=[
            pl.BlockSpec((1, tq, HEAD_DIM), lambda b, h, i: (b, i, base + h)),
            pl.BlockSpec((1, seq, HEAD_DIM), lambda b, h, i: (b, 0, base + N_HEADS + h)),
            pl.BlockSpec((1, seq, HEAD_DIM), lambda b, h, i: (b, 0, base + 2 * N_HEADS + h)),
            pl.BlockSpec((1, tq, HEAD_DIM), lambda b, h, i: (b, i, 2 * N_HEADS + h)),
        ],
        out_specs=pl.BlockSpec((1, tq, HEAD_DIM), lambda b, h, i: (b, i, h)),
        out_shape=jax.ShapeDtypeStruct((batch, seq, BRANCH_W), bf16),
        scratch_shapes=[
            pltpu.VMEM((tq, LANES), f32),
            pltpu.VMEM((tq, HEAD_DIM), f32),
        ],
        compiler_params=_cparams(("parallel", "parallel", "arbitrary")),
        name="stick_attention",
    )(qkv3, qkv3, qkv3, gs3)


def _tail_kernel(x_ref, hn_ref, ya_ref, yb_ref, yc_ref, wg0_ref, wg1_ref, wg2_ref,
                 bg0_ref, bg1_ref, bg2_ref, wup_ref, wout_ref, o_ref):
    j = pl.program_id(1)

    @pl.when(j == 0)
    def _():
        o_ref[...] = x_ref[...]

    hn = hn_ref[...]
    merged = None
    branches = ((ya_ref, wg0_ref, bg0_ref), (yb_ref, wg1_ref, bg1_ref), (yc_ref, wg2_ref, bg2_ref))
    for n, (y_ref, wg_ref, bg_ref) in enumerate(branches):
        gl = jnp.dot(hn, wg_ref[...], preferred_element_type=f32) + bg_ref[...]
        gate = 1.0 / (1.0 + jnp.exp(-gl))
        up = jnp.dot(y_ref[...], wup_ref[n], preferred_element_type=f32)
        merged = gate * up if merged is None else merged + gate * up
    o_ref[...] += jnp.dot(merged.astype(bf16), wout_ref[...], preferred_element_type=f32)


def _tail(x2, hn, ya, yb, yc, w_mg, b_mg, w_up, w_out, layer):
    m, d = x2.shape
    tm, tn = TAIL_TM, TAIL_TN
    nj = d // tn

    def wg_spec(n):
        return pl.BlockSpec((None, d, tn), lambda i, j: (layer, 0, n * nj + j))

    def bg_spec(n):
        return pl.BlockSpec((None, 1, tn), lambda i, j: (layer, 0, n * nj + j))

    y_spec = pl.BlockSpec((tm, BRANCH_W), lambda i, j: (i, 0))
    return pl.pallas_call(
        _tail_kernel,
        grid=(m // tm, nj),
        in_specs=[
            pl.BlockSpec((tm, d), lambda i, j: (i, 0)),
            pl.BlockSpec((tm, d), lambda i, j: (i, 0)),
            y_spec, y_spec, y_spec,
            wg_spec(0), wg_spec(1), wg_spec(2),
            bg_spec(0), bg_spec(1), bg_spec(2),
            pl.BlockSpec((None, N_BRANCH, BRANCH_W, tn), lambda i, j: (layer, 0, 0, j)),
            pl.BlockSpec((None, tn, d), lambda i, j: (layer, j, 0)),
        ],
        out_specs=pl.BlockSpec((tm, d), lambda i, j: (i, 0)),
        out_shape=jax.ShapeDtypeStruct((m, d), f32),
        compiler_params=_cparams(("parallel", "arbitrary")),
        name="merge_tail",
    )(x2, hn, ya, yb, yc, w_mg, w_mg, w_mg, b_mg, b_mg, b_mg, w_up, w_out)


def _rel_bias_table(rel_bias):
    n_h = rel_bias.shape[0]
    far = BAND - 1 - MAX_REL
    t = jnp.concatenate([rel_bias[:, MAX_REL - (CHUNK - 1):],
                         jnp.broadcast_to(rel_bias[:, -1:], (n_h, far))], axis=1)
    trev = t[:, ::-1]
    rows = [trev[:, CHUNK - 1 - qi:CHUNK - 1 - qi + BAND] for qi in range(CHUNK)]
    return jnp.stack(rows, axis=1)


def _layer(x2, batch, seq, layer, norm_g3, w16, wf_t, w_mg, b_mg, w_up16, w_out16, b_f, qk_norm_g,
           rel_bias):
    scale2 = HEAD_DIM ** -0.5 * LOG2E
    ones = jnp.ones((BRANCH_W,), f32)
    gq = lambda g: jnp.tile(g, N_HEADS)
    gains = jnp.stack([
        jnp.concatenate([gq(qk_norm_g[0]) * scale2, gq(qk_norm_g[1]), ones]),
        jnp.concatenate([gq(qk_norm_g[2]) * scale2, gq(qk_norm_g[3]), ones]),
        jnp.concatenate([ones * scale2, ones, ones]),
    ]).reshape(N_BRANCH, 1, 3 * BRANCH_W)
    bias2 = _rel_bias_table(rel_bias) * LOG2E

    qkv, gsilu, hn, f_t = _proj(x2, norm_g3, w16, wf_t, gains, layer)
    qkv3 = qkv.reshape(batch, seq, -1)
    gs3 = gsilu.reshape(batch, seq, -1)

    c = _forget_cumsum(f_t, b_f, batch, seq)
    cb = jnp.max(-c, axis=-1).reshape(-1)
    zb = (HEAD_DIM * scale2 * BF16_ROUND ** 2 * jnp.max(jnp.abs(qk_norm_g[0]))
          * jnp.max(jnp.abs(qk_norm_g[1]))).reshape(1)

    ya = _fox_attention(qkv3, c, cb, zb, gs3).reshape(batch * seq, BRANCH_W)
    yb = _chunk_attention(qkv3, bias2, gs3).reshape(batch * seq, BRANCH_W)
    yc = _stick_attention(qkv3, gs3).reshape(batch * seq, BRANCH_W)
    return _tail(x2, hn, ya, yb, yc, w_mg, b_mg, w_up16, w_out16, layer)


def kernel(x, norm_g, w_in, b_f, b_gate, qk_norm_g, rel_bias, w_up, w_out):
    batch, seq, d = x.shape
    depth = norm_g.shape[0]
    qkvg = N_BRANCH * 4 * BRANCH_W
    x2 = x.reshape(batch * seq, d)
    w16 = w_in.astype(bf16)
    w_mg = w16[:, :, qkvg + N_HEADS:]
    wf = jnp.swapaxes(w_in[:, :, qkvg:qkvg + N_HEADS], 1, 2)
    wf_t = jnp.pad(wf, ((0, 0), (0, 8 - N_HEADS), (0, 0))).astype(bf16)
    b_mg = b_gate.reshape(depth, 1, N_BRANCH * d)
    w_up16 = w_up.astype(bf16)
    w_out16 = w_out.astype(bf16)
    norm_g3 = norm_g.reshape(depth, 1, d)
    for layer in range(depth):
        x2 = _layer(x2, batch, seq, layer, norm_g3, w16, wf_t, w_mg, b_mg, w_up16, w_out16,
                    b_f[layer], qk_norm_g[layer], rel_bias[layer])
    return x2.reshape(batch, seq, d)
```

```python
import math

import jax
import jax.numpy as jnp
from jax import lax
from jax.experimental import pallas as pl
from jax.experimental.pallas import tpu as pltpu

N_HEADS = 4
HEAD_DIM = 128
BRANCH_W = N_HEADS * HEAD_DIM
N_BRANCH = 3
CHUNK = 64
LEFT_CHUNKS = 8
BAND = (LEFT_CHUNKS + 1) * CHUNK
MAX_REL = 128
EPS = 1e-6
NEG = -1e30
LOG2E = math.log2(math.e)

LANES = 128
VMEM_LIMIT = 56 * 1024 * 1024

PROJ_TM = 512
ATT_TQ = 512
FOX_TK = 512
SB_TK = 256
CHUNK_SPLIT = 2
TAIL_TM = 512
TAIL_TN = 512

UNDERFLOW_LOG2 = 152.0
BF16_ROUND = 1.0 + 2.0 ** -8

f32 = jnp.float32
bf16 = jnp.bfloat16


def _cparams(sem):
    return pltpu.CompilerParams(dimension_semantics=sem, vmem_limit_bytes=VMEM_LIMIT)


def _dot_nt(a, b):
    return lax.dot_general(a, b, (((1,), (1,)), ((), ())), preferred_element_type=f32)


def _lane_tile(a, n):
    return a if n == 1 else jnp.concatenate([a] * n, axis=1)


def _wprep_kernel(a_ref, g_ref, t_ref, wa_ref, wg_ref):
    wa_ref[...] = a_ref[...].astype(bf16)
    cat = jnp.concatenate([g_ref[...], t_ref[...]], axis=1)
    wg_ref[...] = cat[:, N_HEADS:N_HEADS + g_ref.shape[1]].astype(bf16)


def _weight_prep(w_in):
    depth, d, _ = w_in.shape
    qkvg = N_BRANCH * 4 * BRANCH_W
    tn = BRANCH_W
    nj = qkvg // tn
    out = jax.ShapeDtypeStruct((depth, d, qkvg), bf16)
    return pl.pallas_call(
        _wprep_kernel,
        grid=(depth, nj),
        in_specs=[
            pl.BlockSpec((None, d, tn), lambda l, j: (l, 0, j)),
            pl.BlockSpec((None, d, tn), lambda l, j: (l, 0, nj + j)),
            pl.BlockSpec((None, d, LANES), lambda l, j: (l, 0, (qkvg + (j + 1) * tn) // LANES)),
        ],
        out_specs=[
            pl.BlockSpec((None, d, tn), lambda l, j: (l, 0, j)),
            pl.BlockSpec((None, d, tn), lambda l, j: (l, 0, j)),
        ],
        out_shape=[out, out],
        compiler_params=_cparams(("parallel", "parallel")),
        name="weight_prep",
    )(w_in, w_in, w_in)


def _proj_kernel(x_ref, g_ref, wq_ref, wk_ref, wv_ref, wg_ref, wf_ref, gain_ref,
                 qkv_ref, gs_ref, hn_ref, ft_ref, hn_s):
    j = pl.program_id(1)

    @pl.when(j == 0)
    def _():
        x = x_ref[...]
        ms = jnp.mean(x * x, axis=-1, keepdims=True)
        hb = (x * lax.rsqrt(ms + EPS) * g_ref[...]).astype(bf16)
        hn_s[...] = hb
        hn_ref[...] = hb
        ft_ref[...] = _dot_nt(wf_ref[...], hb)

    hn = hn_s[...]
    gain = gain_ref[...]

    qk_normed = j < 2
    for part, w_ref in enumerate((wq_ref, wk_ref)):
        acc = jnp.dot(hn, w_ref[...], preferred_element_type=f32)
        for c0 in range(0, BRANCH_W, HEAD_DIM):
            a = acc[:, c0:c0 + HEAD_DIM]
            rs = lax.rsqrt(jnp.mean(a * a, axis=-1, keepdims=True) + EPS)
            rs = jnp.where(qk_normed, rs, 1.0)
            col = part * BRANCH_W + c0
            qkv_ref[:, col:col + HEAD_DIM] = (a * rs * gain[:, col:col + HEAD_DIM]).astype(bf16)

    qkv_ref[:, 2 * BRANCH_W:] = jnp.dot(hn, wv_ref[...], preferred_element_type=f32).astype(bf16)
    a = jnp.dot(hn, wg_ref[...], preferred_element_type=f32)
    gs_ref[...] = a / (1.0 + jnp.exp(-a))


def _proj(x2, norm_g3, w16, wf_t, gains, layer):
    m, d = x2.shape
    tm = PROJ_TM

    def w_spec(part):
        return pl.BlockSpec((None, d, BRANCH_W), lambda i, j: (layer, 0, 4 * j + part))

    return pl.pallas_call(
        _proj_kernel,
        grid=(m // tm, N_BRANCH),
        in_specs=[
            pl.BlockSpec((tm, d), lambda i, j: (i, 0)),
            pl.BlockSpec((None, 1, d), lambda i, j: (layer, 0, 0)),
            w_spec(0), w_spec(1), w_spec(2), w_spec(3),
            pl.BlockSpec((None, 8, d), lambda i, j: (layer, 0, 0)),
            pl.BlockSpec((None, 1, 3 * BRANCH_W), lambda i, j: (j, 0, 0)),
        ],
        out_specs=[
            pl.BlockSpec((tm, 3 * BRANCH_W), lambda i, j: (i, j)),
            pl.BlockSpec((tm, BRANCH_W), lambda i, j: (i, j)),
            pl.BlockSpec((tm, d), lambda i, j: (i, 0)),
            pl.BlockSpec((8, tm), lambda i, j: (0, i)),
        ],
        out_shape=[
            jax.ShapeDtypeStruct((m, N_BRANCH * 3 * BRANCH_W), bf16),
            jax.ShapeDtypeStruct((m, N_BRANCH * BRANCH_W), f32),
            jax.ShapeDtypeStruct((m, d), bf16),
            jax.ShapeDtypeStruct((8, m), f32),
        ],
        scratch_shapes=[pltpu.VMEM((tm, d), bf16)],
        compiler_params=_cparams(("parallel", "arbitrary")),
        name="proj",
    )(x2, norm_g3, w16, w16, w16, w16, wf_t, gains)


def _cumsum_kernel(bf_ref, f_ref, c_ref):
    h = pl.program_id(0)
    x = f_ref[0, 0] + bf_ref[h]
    lf = jnp.minimum(x, 0.0) - jnp.log(1.0 + jnp.exp(-jnp.abs(x)))
    r = lf.shape[0]
    hp = lax.Precision.HIGHEST
    row = lax.broadcasted_iota(jnp.int32, (LANES, LANES), 0)
    col = lax.broadcasted_iota(jnp.int32, (LANES, LANES), 1)
    upper = (row <= col).astype(f32)
    within = jnp.dot(lf, upper, precision=hp, preferred_element_type=f32)
    tot = jnp.broadcast_to(within[:, LANES - 1:LANES], (r, LANES))
    rr = lax.broadcasted_iota(jnp.int32, (r, r), 0)
    rc = lax.broadcasted_iota(jnp.int32, (r, r), 1)
    earlier = (rc < rr).astype(f32)
    offs = jnp.dot(earlier, tot, precision=hp, preferred_element_type=f32)
    c_ref[0, 0] = (within + offs) * LOG2E


def _forget_cumsum(f_t, b_f, batch, seq):
    r = seq // LANES
    f4 = f_t[:N_HEADS].reshape(N_HEADS, batch, r, LANES)
    c = pl.pallas_call(
        _cumsum_kernel,
        grid=(N_HEADS, batch),
        in_specs=[
            pl.BlockSpec(memory_space=pltpu.SMEM),
            pl.BlockSpec((1, 1, r, LANES), lambda h, b: (h, b, 0, 0)),
        ],
        out_specs=pl.BlockSpec((1, 1, r, LANES), lambda h, b: (h, b, 0, 0)),
        out_shape=jax.ShapeDtypeStruct((N_HEADS, batch, r, LANES), f32),
        compiler_params=_cparams(("parallel", "parallel")),
        name="forget_cumsum",
    )(b_f, f4)
    return c.reshape(N_HEADS, batch, seq // FOX_TK, FOX_TK)


def _fox_kernel(zb_ref, cb_ref, q_ref, k_ref, v_ref, c_ref, g_ref, o_ref, m_s, l_s, acc_s, s_s):
    b = pl.program_id(0)
    h = pl.program_id(1)
    i = pl.program_id(2)
    tq, tk = ATT_TQ, FOX_TK
    nt = pl.num_programs(2)
    rep = tk // LANES

    m_s[...] = jnp.full(m_s.shape, NEG, f32)
    l_s[...] = jnp.zeros(l_s.shape, f32)
    acc_s[...] = jnp.zeros(acc_s.shape, f32)

    q = q_ref[0]

    def logits(j):
        jj = jnp.maximum(j, 0)
        off = pl.multiple_of(jj * tk, tk)
        return _dot_nt(q, k_ref[0, pl.ds(off, tk), :]) - c_ref[0, 0, pl.ds(jj, 1), :]

    def step(j, s, masked):
        s_s[...] = logits(j - 1)
        off = pl.multiple_of(j * tk, tk)
        v = v_ref[0, pl.ds(off, tk), :]
        if masked:
            row = lax.broadcasted_iota(jnp.int32, (tq, tk), 0)
            col = lax.broadcasted_iota(jnp.int32, (tq, tk), 1)
            s = jnp.where(col <= row, s, NEG)
        m_old = m_s[...]
        m_new = jnp.maximum(m_old, jnp.max(s, axis=-1, keepdims=True))
        alpha = jnp.exp2(m_old - m_new)
        p = jnp.exp2(s - _lane_tile(m_new, rep))
        lsum = p[:, 0:LANES]
        for cblk in range(1, rep):
            lsum = lsum + p[:, cblk * LANES:(cblk + 1) * LANES]
        l_s[...] = alpha * l_s[...] + lsum
        acc_s[...] = alpha * acc_s[...] + jnp.dot(p.astype(bf16), v, preferred_element_type=f32)
        m_s[...] = m_new

    step(i, logits(i), True)

    thr = jnp.min(m_s[...]) - zb_ref[0] - UNDERFLOW_LOG2
    base = (h * pl.num_programs(0) + b) * nt

    def cond(j):
        return jnp.logical_and(j >= 0, cb_ref[base + jnp.maximum(j, 0)] > thr)

    def body(j):
        step(j, s_s[...], False)
        return j - 1

    lax.while_loop(cond, body, i - 1)
    l = jnp.sum(l_s[...], axis=-1, keepdims=True)
    o_ref[0] = (acc_s[...] / l * g_ref[0]).astype(bf16)


def _fox_attention(qkv3, c, cb, zb, gs3):
    batch, seq, _ = qkv3.shape
    tq = ATT_TQ
    assert FOX_TK == tq
    return pl.pallas_call(
        _fox_kernel,
        grid=(batch, N_HEADS, seq // tq),
        in_specs=[
            pl.BlockSpec(memory_space=pltpu.SMEM),
            pl.BlockSpec(memory_space=pltpu.SMEM),
            pl.BlockSpec((1, tq, HEAD_DIM), lambda b, h, i: (b, i, h)),
            pl.BlockSpec((1, seq, HEAD_DIM), lambda b, h, i: (b, 0, N_HEADS + h)),
            pl.BlockSpec((1, seq, HEAD_DIM), lambda b, h, i: (b, 0, 2 * N_HEADS + h)),
            pl.BlockSpec((1, 1, seq // FOX_TK, FOX_TK), lambda b, h, i: (h, b, 0, 0)),
            pl.BlockSpec((1, tq, HEAD_DIM), lambda b, h, i: (b, i, h)),
        ],
        out_specs=pl.BlockSpec((1, tq, HEAD_DIM), lambda b, h, i: (b, i, h)),
        out_shape=jax.ShapeDtypeStruct((batch, seq, BRANCH_W), bf16),
        scratch_shapes=[
            pltpu.VMEM((tq, LANES), f32),
            pltpu.VMEM((tq, LANES), f32),
            pltpu.VMEM((tq, HEAD_DIM), f32),
            pltpu.VMEM((tq, FOX_TK), f32),
        ],
        compiler_params=_cparams(("parallel", "parallel", "arbitrary")),
        name="fox_attention",
    )(zb, cb, qkv3, qkv3, qkv3, c, gs3)


def _chunk_kernel(q_ref, kp_ref, kc_ref, vp_ref, vc_ref, bias_ref, g_ref, o_ref):
    i = pl.program_id(2)
    span = LEFT_CHUNKS * CHUNK
    rows = span // CHUNK_SPLIT
    kp, kc, vp, vc = kp_ref[0], kc_ref[0], vp_ref[0], vc_ref[0]
    for r0 in range(0, span, rows):
        q = q_ref[0, r0:r0 + rows, :]
        s_prev = _dot_nt(q, kp) + bias_ref[0, r0:r0 + rows, 0:span]
        s_cur = _dot_nt(q, kc) + bias_ref[0, r0:r0 + rows, span:2 * span]
        s_prev = jnp.where(i > 0, s_prev, NEG)
        m = jnp.maximum(jnp.max(s_prev, axis=-1, keepdims=True),
                        jnp.max(s_cur, axis=-1, keepdims=True))
        p_prev = jnp.exp2(s_prev - m)
        p_cur = jnp.exp2(s_cur - m)
        l = jnp.sum(p_prev, axis=-1, keepdims=True) + jnp.sum(p_cur, axis=-1, keepdims=True)
        o = (jnp.dot(p_prev.astype(bf16), vp, preferred_element_type=f32)
             + jnp.dot(p_cur.astype(bf16), vc, preferred_element_type=f32))
        o_ref[0, r0:r0 + rows, :] = (o / l * g_ref[0, r0:r0 + rows, :]).astype(bf16)


def _chunk_attention(qkv3, bias_full, gs3):
    batch, seq, _ = qkv3.shape
    span = LEFT_CHUNKS * CHUNK
    base = 3 * N_HEADS

    def prev(b, h, i, seg):
        return (b, jnp.maximum(i - 1, 0), base + seg * N_HEADS + h)

    return pl.pallas_call(
        _chunk_kernel,
        grid=(batch, N_HEADS, seq // span),
        in_specs=[
            pl.BlockSpec((1, span, HEAD_DIM), lambda b, h, i: (b, i, base + h)),
            pl.BlockSpec((1, span, HEAD_DIM), lambda b, h, i: prev(b, h, i, 1)),
            pl.BlockSpec((1, span, HEAD_DIM), lambda b, h, i: (b, i, base + N_HEADS + h)),
            pl.BlockSpec((1, span, HEAD_DIM), lambda b, h, i: prev(b, h, i, 2)),
            pl.BlockSpec((1, span, HEAD_DIM), lambda b, h, i: (b, i, base + 2 * N_HEADS + h)),
            pl.BlockSpec((1, span, 2 * span), lambda b, h, i: (h, 0, 0)),
            pl.BlockSpec((1, span, HEAD_DIM), lambda b, h, i: (b, i, N_HEADS + h)),
        ],
        out_specs=pl.BlockSpec((1, span, HEAD_DIM), lambda b, h, i: (b, i, h)),
        out_shape=jax.ShapeDtypeStruct((batch, seq, BRANCH_W), bf16),
        compiler_params=_cparams(("parallel", "parallel", "arbitrary")),
        name="chunk_attention",
    )(qkv3, qkv3, qkv3, qkv3, qkv3, bias_full, gs3)


def _stick_kernel(q_ref, k_ref, v_ref, g_ref, o_ref, carry_s, acc_s):
    i = pl.program_id(2)
    tq, tk = ATT_TQ, SB_TK
    per = tq // tk
    rep = tk // LANES
    q = q_ref[0]
    carry_s[...] = jnp.zeros(carry_s.shape, f32)
    acc_s[...] = jnp.zeros(acc_s.shape, f32)

    kr = lax.broadcasted_iota(jnp.int32, (2 * tk, tk), 0)
    kc = lax.broadcasted_iota(jnp.int32, (2 * tk, tk), 1)
    kr = jnp.where(kr >= tk, kr - tk, kr)
    suffix = (kr >= kc).astype(bf16)

    def step(j, masked):
        off = pl.multiple_of(j * tk, tk)
        k = k_ref[0, pl.ds(off, tk), :]
        v = v_ref[0, pl.ds(off, tk), :]
        z = _dot_nt(q, k)
        p = jnp.maximum(z, 0.0) + jnp.log2(1.0 + jnp.exp2(-jnp.abs(z)))
        if masked:
            row = i * tq + lax.broadcasted_iota(jnp.int32, (tq, tk), 0)
            col = j * tk + lax.broadcasted_iota(jnp.int32, (tq, tk), 1)
            keep = col < row
            p = jnp.where(keep, p, 0.0)
        hi = p.astype(bf16)
        lo = (p - hi.astype(f32)).astype(bf16)
        rp = jnp.dot(jnp.concatenate([hi, lo], axis=1), suffix, preferred_element_type=f32)
        carry = carry_s[...]
        a = jnp.exp2(z - rp - _lane_tile(carry, rep))
        if masked:
            a = jnp.where(keep, a, 0.0)
        acc_s[...] += jnp.dot(a.astype(bf16), v, preferred_element_type=f32)
        carry_s[...] = carry + jnp.broadcast_to(rp[:, 0:1], (tq, LANES))

    last = i * per + per - 1
    for d in range(per):
        step(last - d, True)

    def cond(state):
        j, cmin = state
        return jnp.logical_and(j >= 0, cmin < UNDERFLOW_LOG2)

    def body(state):
        j, _ = state
        step(j, False)
        return j - 1, jnp.min(carry_s[...])

    lax.while_loop(cond, body, (i * per - 1, jnp.min(carry_s[...])))
    o_ref[0] = (acc_s[...] * g_ref[0]).astype(bf16)


def _stick_attention(qkv3, gs3):
    batch, seq, _ = qkv3.shape
    tq = ATT_TQ
    base = 6 * N_HEADS
    return pl.pallas_call(
        _stick_kernel,
        grid=(batch, N_HEADS, seq // tq),
        in_specs=[
            pl.BlockSpec((1, tq, HEAD_DIM), lambda b, h, i: (b, i, base + h)),
            pl.BlockSpec((1, seq, HEAD_DIM), lambda b, h, i: (b, 0, base + N_HEADS + h)),
            pl.BlockSpec((1, seq, HEAD_DIM), lambda b, h, i: (b, 0, base + 2 * N_HEADS + h)),
            pl.BlockSpec((1, tq, HEAD_DIM), lambda b, h, i: (b, i, 2 * N_HEADS + h)),
        ],
        out_specs=pl.BlockSpec((1, tq, HEAD_DIM), lambda b, h, i: (b, i, h)),
        out_shape=jax.ShapeDtypeStruct((batch, seq, BRANCH_W), bf16),
        scratch_shapes=[
            pltpu.VMEM((tq, LANES), f32),
            pltpu.VMEM((tq, HEAD_DIM), f32),
        ],
        compiler_params=_cparams(("parallel", "parallel", "arbitrary")),
        name="stick_attention",
    )(qkv3, qkv3, qkv3, gs3)


def _tail_kernel(x_ref, hn_ref, ya_ref, yb_ref, yc_ref, wg0_ref, wg1_ref, wg2_ref,
                 bg0_ref, bg1_ref, bg2_ref, wup_ref, wout_ref, o_ref):
    j = pl.program_id(1)

    @pl.when(j == 0)
    def _():
        o_ref[...] = x_ref[...]

    hn = hn_ref[...]
    merged = None
    branches = ((ya_ref, wg0_ref, bg0_ref), (yb_ref, wg1_ref, bg1_ref), (yc_ref, wg2_ref, bg2_ref))
    for n, (y_ref, wg_ref, bg_ref) in enumerate(branches):
        gl = jnp.dot(hn, wg_ref[...], preferred_element_type=f32) + bg_ref[...]
        gate = 1.0 / (1.0 + jnp.exp(-gl))
        up = jnp.dot(y_ref[...], wup_ref[n], preferred_element_type=f32)
        merged = gate * up if merged is None else merged + gate * up
    o_ref[...] += jnp.dot(merged.astype(bf16), wout_ref[...], preferred_element_type=f32)


def _tail(x2, hn, ya, yb, yc, w_mg, b_mg, w_up, w_out, layer):
    m, d = x2.shape
    tm, tn = TAIL_TM, TAIL_TN
    nj = d // tn

    def wg_spec(n):
        return pl.BlockSpec((None, d, tn), lambda i, j: (layer, 0, n * nj + j))

    def bg_spec(n):
        return pl.BlockSpec((None, 1, tn), lambda i, j: (layer, 0, n * nj + j))

    y_spec = pl.BlockSpec((tm, BRANCH_W), lambda i, j: (i, 0))
    return pl.pallas_call(
        _tail_kernel,
        grid=(m // tm, nj),
        in_specs=[
            pl.BlockSpec((tm, d), lambda i, j: (i, 0)),
            pl.BlockSpec((tm, d), lambda i, j: (i, 0)),
            y_spec, y_spec, y_spec,
            wg_spec(0), wg_spec(1), wg_spec(2),
            bg_spec(0), bg_spec(1), bg_spec(2),
            pl.BlockSpec((None, N_BRANCH, BRANCH_W, tn), lambda i, j: (layer, 0, 0, j)),
            pl.BlockSpec((None, tn, d), lambda i, j: (layer, j, 0)),
        ],
        out_specs=pl.BlockSpec((tm, d), lambda i, j: (i, 0)),
        out_shape=jax.ShapeDtypeStruct((m, d), f32),
        compiler_params=_cparams(("parallel", "arbitrary")),
        name="merge_tail",
    )(x2, hn, ya, yb, yc, w_mg, w_mg, w_mg, b_mg, b_mg, b_mg, w_up, w_out)


def _rel_bias_table(rel_bias):
    n_h = rel_bias.shape[0]
    far = BAND - 1 - MAX_REL
    t = jnp.concatenate([rel_bias[:, MAX_REL - (CHUNK - 1):],
                         jnp.broadcast_to(rel_bias[:, -1:], (n_h, far))], axis=1)
    trev = t[:, ::-1]
    rows = [trev[:, CHUNK - 1 - qi:CHUNK - 1 - qi + BAND] for qi in range(CHUNK)]
    band = jnp.stack(rows, axis=1) * LOG2E
    span = LEFT_CHUNKS * CHUNK
    tiles = [jnp.pad(band, ((0, 0), (0, 0), (c * CHUNK, 2 * span - BAND - c * CHUNK)),
                     constant_values=NEG) for c in range(LEFT_CHUNKS)]
    return jnp.concatenate(tiles, axis=1)


def _layer(x2, batch, seq, layer, norm_g3, w16, wf_t, w_mg, b_mg, w_up16, w_out16, b_f, qk_norm_g,
           rel_bias):
    scale2 = HEAD_DIM ** -0.5 * LOG2E
    ones = jnp.ones((BRANCH_W,), f32)
    gq = lambda g: jnp.tile(g, N_HEADS)
    gains = jnp.stack([
        jnp.concatenate([gq(qk_norm_g[0]) * scale2, gq(qk_norm_g[1]), ones]),
        jnp.concatenate([gq(qk_norm_g[2]) * scale2, gq(qk_norm_g[3]), ones]),
        jnp.concatenate([ones * scale2, ones, ones]),
    ]).reshape(N_BRANCH, 1, 3 * BRANCH_W)
    bias_full = _rel_bias_table(rel_bias)

    qkv, gsilu, hn, f_t = _proj(x2, norm_g3, w16, wf_t, gains, layer)
    qkv3 = qkv.reshape(batch, seq, -1)
    gs3 = gsilu.reshape(batch, seq, -1)

    c = _forget_cumsum(f_t, b_f, batch, seq)
    cb = jnp.max(-c, axis=-1).reshape(-1)
    zb = (HEAD_DIM * scale2 * BF16_ROUND ** 2 * jnp.max(jnp.abs(qk_norm_g[0]))
          * jnp.max(jnp.abs(qk_norm_g[1]))).reshape(1)

    ya = _fox_attention(qkv3, c, cb, zb, gs3).reshape(batch * seq, BRANCH_W)
    yb = _chunk_attention(qkv3, bias_full, gs3).reshape(batch * seq, BRANCH_W)
    yc = _stick_attention(qkv3, gs3).reshape(batch * seq, BRANCH_W)
    return _tail(x2, hn, ya, yb, yc, w_mg, b_mg, w_up16, w_out16, layer)


def kernel(x, norm_g, w_in, b_f, b_gate, qk_norm_g, rel_bias, w_up, w_out):
    batch, seq, d = x.shape
    depth = norm_g.shape[0]
    qkvg = N_BRANCH * 4 * BRANCH_W
    x2 = x.reshape(batch * seq, d)
    w16, w_mg = _weight_prep(w_in)
    wf = jnp.swapaxes(w_in[:, :, qkvg:qkvg + N_HEADS], 1, 2)
    wf_t = jnp.pad(wf, ((0, 0), (0, 8 - N_HEADS), (0, 0))).astype(bf16)
    b_mg = b_gate.reshape(depth, 1, N_BRANCH * d)
    w_up16 = w_up.astype(bf16)
    w_out16 = w_out.astype(bf16)
    norm_g3 = norm_g.reshape(depth, 1, d)
    for layer in range(depth):
        x2 = _layer(x2, batch, seq, layer, norm_g3, w16, wf_t, w_mg, b_mg, w_up16, w_out16,
                    b_f[layer], qk_norm_g[layer], rel_bias[layer])
    return x2.reshape(batch, seq, d)
```

```python
import math

import jax
import jax.numpy as jnp
from jax import lax
from jax.experimental import pallas as pl
from jax.experimental.pallas import tpu as pltpu

N_HEADS = 4
HEAD_DIM = 128
BRANCH_W = N_HEADS * HEAD_DIM
N_BRANCH = 3
CHUNK = 64
LEFT_CHUNKS = 8
BAND = (LEFT_CHUNKS + 1) * CHUNK
MAX_REL = 128
EPS = 1e-6
NEG = -1e30
LOG2E = math.log2(math.e)

LANES = 128
VMEM_LIMIT = 56 * 1024 * 1024

PROJ_TM = 512
ATT_TQ = 512
FOX_TK = 512
SB_TQ = 512
SB_TK = 256
CHUNK_SPLIT = 2
TAIL_TM = 512
TAIL_TN = 512

UNDERFLOW_LOG2 = 152.0
BF16_ROUND = 1.0 + 2.0 ** -8

f32 = jnp.float32
bf16 = jnp.bfloat16


def _cparams(sem):
    return pltpu.CompilerParams(dimension_semantics=sem, vmem_limit_bytes=VMEM_LIMIT)


def _dot_nt(a, b):
    return lax.dot_general(a, b, (((1,), (1,)), ((), ())), preferred_element_type=f32)


def _lane_tile(a, n):
    return a if n == 1 else jnp.concatenate([a] * n, axis=1)


def _proj_kernel(x_ref, g_ref, wq_ref, wk_ref, wv_ref, wg_ref, wf_ref, gain_ref,
                 qkv_ref, gs_ref, hn_ref, ft_ref, hn_s):
    j = pl.program_id(1)

    @pl.when(j == 0)
    def _():
        x = x_ref[...]
        ms = jnp.mean(x * x, axis=-1, keepdims=True)
        hb = (x * lax.rsqrt(ms + EPS) * g_ref[...]).astype(bf16)
        hn_s[...] = hb
        hn_ref[...] = hb
        ft_ref[...] = _dot_nt(wf_ref[...], hb)

    hn = hn_s[...]
    gain = gain_ref[...]

    qk_normed = j < 2
    for part, w_ref in enumerate((wq_ref, wk_ref)):
        acc = jnp.dot(hn, w_ref[...], preferred_element_type=f32)
        for c0 in range(0, BRANCH_W, HEAD_DIM):
            a = acc[:, c0:c0 + HEAD_DIM]
            rs = lax.rsqrt(jnp.mean(a * a, axis=-1, keepdims=True) + EPS)
            rs = jnp.where(qk_normed, rs, 1.0)
            col = part * BRANCH_W + c0
            qkv_ref[:, col:col + HEAD_DIM] = (a * rs * gain[:, col:col + HEAD_DIM]).astype(bf16)

    qkv_ref[:, 2 * BRANCH_W:] = jnp.dot(hn, wv_ref[...], preferred_element_type=f32).astype(bf16)
    a = jnp.dot(hn, wg_ref[...], preferred_element_type=f32)
    gs_ref[...] = a / (1.0 + jnp.exp(-a))


def _proj(x2, norm_g3, w16, wf_t, gains, layer):
    m, d = x2.shape
    tm = PROJ_TM

    def w_spec(part):
        return pl.BlockSpec((None, d, BRANCH_W), lambda i, j: (layer, 0, 4 * j + part))

    return pl.pallas_call(
        _proj_kernel,
        grid=(m // tm, N_BRANCH),
        in_specs=[
            pl.BlockSpec((tm, d), lambda i, j: (i, 0)),
            pl.BlockSpec((None, 1, d), lambda i, j: (layer, 0, 0)),
            w_spec(0), w_spec(1), w_spec(2), w_spec(3),
            pl.BlockSpec((None, 8, d), lambda i, j: (layer, 0, 0)),
            pl.BlockSpec((None, 1, 3 * BRANCH_W), lambda i, j: (j, 0, 0)),
        ],
        out_specs=[
            pl.BlockSpec((tm, 3 * BRANCH_W), lambda i, j: (i, j)),
            pl.BlockSpec((tm, BRANCH_W), lambda i, j: (i, j)),
            pl.BlockSpec((tm, d), lambda i, j: (i, 0)),
            pl.BlockSpec((8, tm), lambda i, j: (0, i)),
        ],
        out_shape=[
            jax.ShapeDtypeStruct((m, N_BRANCH * 3 * BRANCH_W), bf16),
            jax.ShapeDtypeStruct((m, N_BRANCH * BRANCH_W), f32),
            jax.ShapeDtypeStruct((m, d), bf16),
            jax.ShapeDtypeStruct((8, m), f32),
        ],
        scratch_shapes=[pltpu.VMEM((tm, d), bf16)],
        compiler_params=_cparams(("parallel", "arbitrary")),
        name="proj",
    )(x2, norm_g3, w16, w16, w16, w16, wf_t, gains)


def _cumsum_kernel(bf_ref, f_ref, c_ref):
    h = pl.program_id(0)
    x = f_ref[0, 0] + bf_ref[h]
    lf = jnp.minimum(x, 0.0) - jnp.log(1.0 + jnp.exp(-jnp.abs(x)))
    r = lf.shape[0]
    hp = lax.Precision.HIGHEST
    row = lax.broadcasted_iota(jnp.int32, (LANES, LANES), 0)
    col = lax.broadcasted_iota(jnp.int32, (LANES, LANES), 1)
    upper = (row <= col).astype(f32)
    within = jnp.dot(lf, upper, precision=hp, preferred_element_type=f32)
    tot = jnp.broadcast_to(within[:, LANES - 1:LANES], (r, LANES))
    rr = lax.broadcasted_iota(jnp.int32, (r, r), 0)
    rc = lax.broadcasted_iota(jnp.int32, (r, r), 1)
    earlier = (rc < rr).astype(f32)
    offs = jnp.dot(earlier, tot, precision=hp, preferred_element_type=f32)
    c_ref[0, 0] = (within + offs) * LOG2E


def _forget_cumsum(f_t, b_f, batch, seq):
    r = seq // LANES
    f4 = f_t[:N_HEADS].reshape(N_HEADS, batch, r, LANES)
    c = pl.pallas_call(
        _cumsum_kernel,
        grid=(N_HEADS, batch),
        in_specs=[
            pl.BlockSpec(memory_space=pltpu.SMEM),
            pl.BlockSpec((1, 1, r, LANES), lambda h, b: (h, b, 0, 0)),
        ],
        out_specs=pl.BlockSpec((1, 1, r, LANES), lambda h, b: (h, b, 0, 0)),
        out_shape=jax.ShapeDtypeStruct((N_HEADS, batch, r, LANES), f32),
        compiler_params=_cparams(("parallel", "parallel")),
        name="forget_cumsum",
    )(b_f, f4)
    return c.reshape(N_HEADS, batch, seq // FOX_TK, FOX_TK)


def _fox_kernel(zb_ref, cb_ref, q_ref, k_ref, v_ref, c_ref, g_ref, o_ref, m_s, l_s, acc_s, s_s):
    b = pl.program_id(0)
    h = pl.program_id(1)
    i = pl.program_id(2)
    tq, tk = ATT_TQ, FOX_TK
    nt = pl.num_programs(2)
    rep = tk // LANES

    m_s[...] = jnp.full(m_s.shape, NEG, f32)
    l_s[...] = jnp.zeros(l_s.shape, f32)
    acc_s[...] = jnp.zeros(acc_s.shape, f32)

    q = q_ref[0]

    def logits(j):
        jj = jnp.maximum(j, 0)
        off = pl.multiple_of(jj * tk, tk)
        return _dot_nt(q, k_ref[0, pl.ds(off, tk), :]) - c_ref[0, 0, pl.ds(jj, 1), :]

    def step(j, s, masked):
        s_s[...] = logits(j - 1)
        off = pl.multiple_of(j * tk, tk)
        v = v_ref[0, pl.ds(off, tk), :]
        if masked:
            row = lax.broadcasted_iota(jnp.int32, (tq, tk), 0)
            col = lax.broadcasted_iota(jnp.int32, (tq, tk), 1)
            s = jnp.where(col <= row, s, NEG)
        m_old = m_s[...]
        m_new = jnp.maximum(m_old, jnp.max(s, axis=-1, keepdims=True))
        alpha = jnp.exp2(m_old - m_new)
        p = jnp.exp2(s - _lane_tile(m_new, rep))
        lsum = p[:, 0:LANES]
        for cblk in range(1, rep):
            lsum = lsum + p[:, cblk * LANES:(cblk + 1) * LANES]
        l_s[...] = alpha * l_s[...] + lsum
        acc_s[...] = alpha * acc_s[...] + jnp.dot(p.astype(bf16), v, preferred_element_type=f32)
        m_s[...] = m_new

    step(i, logits(i), True)

    thr = jnp.min(m_s[...]) - zb_ref[0] - UNDERFLOW_LOG2
    base = (h * pl.num_programs(0) + b) * nt

    def cond(j):
        return jnp.logical_and(j >= 0, cb_ref[base + jnp.maximum(j, 0)] > thr)

    def body(j):
        step(j, s_s[...], False)
        return j - 1

    lax.while_loop(cond, body, i - 1)
    l = jnp.sum(l_s[...], axis=-1, keepdims=True)
    o_ref[0] = (acc_s[...] / l * g_ref[0]).astype(bf16)


def _fox_attention(qkv3, c, cb, zb, gs3):
    batch, seq, _ = qkv3.shape
    tq = ATT_TQ
    assert FOX_TK == tq
    return pl.pallas_call(
        _fox_kernel,
        grid=(batch, N_HEADS, seq // tq),
        in_specs=[
            pl.BlockSpec(memory_space=pltpu.SMEM),
            pl.BlockSpec(memory_space=pltpu.SMEM),
            pl.BlockSpec((1, tq, HEAD_DIM), lambda b, h, i: (b, i, h)),
            pl.BlockSpec((1, seq, HEAD_DIM), lambda b, h, i: (b, 0, N_HEADS + h)),
            pl.BlockSpec((1, seq, HEAD_DIM), lambda b, h, i: (b, 0, 2 * N_HEADS + h)),
            pl.BlockSpec((1, 1, seq // FOX_TK, FOX_TK), lambda b, h, i: (h, b, 0, 0)),
            pl.BlockSpec((1, tq, HEAD_DIM), lambda b, h, i: (b, i, h)),
        ],
        out_specs=pl.BlockSpec((1, tq, HEAD_DIM), lambda b, h, i: (b, i, h)),
        out_shape=jax.ShapeDtypeStruct((batch, seq, BRANCH_W), bf16),
        scratch_shapes=[
            pltpu.VMEM((tq, LANES), f32),
            pltpu.VMEM((tq, LANES), f32),
            pltpu.VMEM((tq, HEAD_DIM), f32),
            pltpu.VMEM((tq, FOX_TK), f32),
        ],
        compiler_params=_cparams(("parallel", "parallel", "arbitrary")),
        name="fox_attention",
    )(zb, cb, qkv3, qkv3, qkv3, c, gs3)


def _chunk_kernel(q_ref, kp_ref, kc_ref, vp_ref, vc_ref, bias_ref, g_ref, o_ref):
    i = pl.program_id(2)
    span = LEFT_CHUNKS * CHUNK
    rows = span // CHUNK_SPLIT
    for r0 in range(0, span, rows):
        q = q_ref[0, r0:r0 + rows, :]
        s_prev = _dot_nt(q, kp_ref[0, r0:span, :]) + bias_ref[0, r0:r0 + rows, r0:span]
        s_cur = (_dot_nt(q, kc_ref[0, 0:r0 + rows, :])
                 + bias_ref[0, r0:r0 + rows, span:span + r0 + rows])
        s_prev = jnp.where(i > 0, s_prev, NEG)
        m = jnp.maximum(jnp.max(s_prev, axis=-1, keepdims=True),
                        jnp.max(s_cur, axis=-1, keepdims=True))
        p_prev = jnp.exp2(s_prev - m)
        p_cur = jnp.exp2(s_cur - m)
        l = jnp.sum(p_prev, axis=-1, keepdims=True) + jnp.sum(p_cur, axis=-1, keepdims=True)
        o = (jnp.dot(p_prev.astype(bf16), vp_ref[0, r0:span, :], preferred_element_type=f32)
             + jnp.dot(p_cur.astype(bf16), vc_ref[0, 0:r0 + rows, :], preferred_element_type=f32))
        o_ref[0, r0:r0 + rows, :] = (o / l * g_ref[0, r0:r0 + rows, :]).astype(bf16)


def _chunk_attention(qkv3, bias_full, gs3):
    batch, seq, _ = qkv3.shape
    span = LEFT_CHUNKS * CHUNK
    base = 3 * N_HEADS

    def prev(b, h, i, seg):
        return (b, jnp.maximum(i - 1, 0), base + seg * N_HEADS + h)

    return pl.pallas_call(
        _chunk_kernel,
        grid=(batch, N_HEADS, seq // span),
        in_specs=[
            pl.BlockSpec((1, span, HEAD_DIM), lambda b, h, i: (b, i, base + h)),
            pl.BlockSpec((1, span, HEAD_DIM), lambda b, h, i: prev(b, h, i, 1)),
            pl.BlockSpec((1, span, HEAD_DIM), lambda b, h, i: (b, i, base + N_HEADS + h)),
            pl.BlockSpec((1, span, HEAD_DIM), lambda b, h, i: prev(b, h, i, 2)),
            pl.BlockSpec((1, span, HEAD_DIM), lambda b, h, i: (b, i, base + 2 * N_HEADS + h)),
            pl.BlockSpec((1, span, 2 * span), lambda b, h, i: (h, 0, 0)),
            pl.BlockSpec((1, span, HEAD_DIM), lambda b, h, i: (b, i, N_HEADS + h)),
        ],
        out_specs=pl.BlockSpec((1, span, HEAD_DIM), lambda b, h, i: (b, i, h)),
        out_shape=jax.ShapeDtypeStruct((batch, seq, BRANCH_W), bf16),
        compiler_params=_cparams(("parallel", "parallel", "arbitrary")),
        name="chunk_attention",
    )(qkv3, qkv3, qkv3, qkv3, qkv3, bias_full, gs3)


def _stick_kernel(q_ref, k_ref, v_ref, g_ref, o_ref, carry_s, acc_s):
    i = pl.program_id(2)
    tq, tk = SB_TQ, SB_TK
    per = tq // tk
    rep = tk // LANES
    q = q_ref[0]

    kr = lax.broadcasted_iota(jnp.int32, (2 * tk, tk), 0)
    kc = lax.broadcasted_iota(jnp.int32, (2 * tk, tk), 1)
    kr = jnp.where(kr >= tk, kr - tk, kr)
    suffix = (kr >= kc).astype(bf16)

    def scores(j, masked):
        jj = jnp.maximum(j, 0)
        off = pl.multiple_of(jj * tk, tk)
        z = _dot_nt(q, k_ref[0, pl.ds(off, tk), :])
        p = jnp.maximum(z, 0.0) + jnp.log2(1.0 + jnp.exp2(-jnp.abs(z)))
        if masked:
            row = i * tq + lax.broadcasted_iota(jnp.int32, (tq, tk), 0)
            col = j * tk + lax.broadcasted_iota(jnp.int32, (tq, tk), 1)
            keep = col < row
        else:
            keep = j >= 0
        p = jnp.where(keep, p, 0.0)
        hi = p.astype(bf16)
        lo = (p - hi.astype(f32)).astype(bf16)
        rp = jnp.dot(jnp.concatenate([hi, lo], axis=1), suffix, preferred_element_type=f32)
        return z - rp, rp[:, 0:1], keep, off

    def weigh(zr, tot, keep, off, carry):
        a = jnp.where(keep, jnp.exp2(zr - _lane_tile(carry, rep)), 0.0)
        pv = jnp.dot(a.astype(bf16), v_ref[0, pl.ds(off, tk), :], preferred_element_type=f32)
        return pv, carry + jnp.broadcast_to(tot, (tq, LANES))

    first = i * per + per - 1
    static = [scores(first - d, d < per) for d in range(per + 1)]
    carry = jnp.zeros((tq, LANES), f32)
    acc = jnp.zeros((tq, HEAD_DIM), f32)
    for zr, tot, keep, off in static:
        pv, carry = weigh(zr, tot, keep, off, carry)
        acc = acc + pv
    carry_s[...] = carry
    acc_s[...] = acc

    def cond(state):
        j, cmin = state
        return jnp.logical_and(j >= 0, cmin < UNDERFLOW_LOG2)

    def body(state):
        j, _ = state
        zr, tot, keep, off = scores(j, False)
        pv, carry = weigh(zr, tot, keep, off, carry_s[...])
        acc_s[...] += pv
        carry_s[...] = carry
        return j - 1, jnp.min(carry)

    lax.while_loop(cond, body, (i * per - 2, jnp.min(carry)))
    o_ref[0] = (acc_s[...] * g_ref[0]).astype(bf16)


def _stick_attention(qkv3, gs3):
    batch, seq, _ = qkv3.shape
    tq = SB_TQ
    base = 6 * N_HEADS
    return pl.pallas_call(
        _stick_kernel,
        grid=(batch, N_HEADS, seq // tq),
        in_specs=[
            pl.BlockSpec((1, tq, HEAD_DIM), lambda b, h, i: (b, i, base + h)),
            pl.BlockSpec((1, seq, HEAD_DIM), lambda b, h, i: (b, 0, base + N_HEADS + h)),
            pl.BlockSpec((1, seq, HEAD_DIM), lambda b, h, i: (b, 0, base + 2 * N_HEADS + h)),
            pl.BlockSpec((1, tq, HEAD_DIM), lambda b, h, i: (b, i, 2 * N_HEADS + h)),
        ],
        out_specs=pl.BlockSpec((1, tq, HEAD_DIM), lambda b, h, i: (b, i, h)),
        out_shape=jax.ShapeDtypeStruct((batch, seq, BRANCH_W), bf16),
        scratch_shapes=[
            pltpu.VMEM((tq, LANES), f32),
            pltpu.VMEM((tq, HEAD_DIM), f32),
        ],
        compiler_params=_cparams(("parallel", "parallel", "arbitrary")),
        name="stick_attention",
    )(qkv3, qkv3, qkv3, gs3)


def _tail_kernel(x_ref, hn_ref, ya_ref, yb_ref, yc_ref, wg0_ref, wg1_ref, wg2_ref,
                 bg0_ref, bg1_ref, bg2_ref, wup_ref, wout_ref, o_ref):
    j = pl.program_id(1)

    @pl.when(j == 0)
    def _():
        o_ref[...] = x_ref[...]

    hn = hn_ref[...]
    merged = None
    branches = ((ya_ref, wg0_ref, bg0_ref), (yb_ref, wg1_ref, bg1_ref), (yc_ref, wg2_ref, bg2_ref))
    for n, (y_ref, wg_ref, bg_ref) in enumerate(branches):
        gl = jnp.dot(hn, wg_ref[...], preferred_element_type=f32) + bg_ref[...]
        gate = 1.0 / (1.0 + jnp.exp(-gl))
        up = jnp.dot(y_ref[...], wup_ref[n], preferred_element_type=f32)
        merged = gate * up if merged is None else merged + gate * up
    o_ref[...] += jnp.dot(merged.astype(bf16), wout_ref[...], preferred_element_type=f32)


def _tail(x2, hn, ya, yb, yc, w_mg, b_mg, w_up, w_out, layer):
    m, d = x2.shape
    tm, tn = TAIL_TM, TAIL_TN
    nj = d // tn

    def wg_spec(n):
        return pl.BlockSpec((None, d, tn), lambda i, j: (layer, 0, n * nj + j))

    def bg_spec(n):
        return pl.BlockSpec((None, 1, tn), lambda i, j: (layer, 0, n * nj + j))

    y_spec = pl.BlockSpec((tm, BRANCH_W), lambda i, j: (i, 0))
    return pl.pallas_call(
        _tail_kernel,
        grid=(m // tm, nj),
        in_specs=[
            pl.BlockSpec((tm, d), lambda i, j: (i, 0)),
            pl.BlockSpec((tm, d), lambda i, j: (i, 0)),
            y_spec, y_spec, y_spec,
            wg_spec(0), wg_spec(1), wg_spec(2),
            bg_spec(0), bg_spec(1), bg_spec(2),
            pl.BlockSpec((None, N_BRANCH, BRANCH_W, tn), lambda i, j: (layer, 0, 0, j)),
            pl.BlockSpec((None, tn, d), lambda i, j: (layer, j, 0)),
        ],
        out_specs=pl.BlockSpec((tm, d), lambda i, j: (i, 0)),
        out_shape=jax.ShapeDtypeStruct((m, d), f32),
        compiler_params=_cparams(("parallel", "arbitrary")),
        name="merge_tail",
    )(x2, hn, ya, yb, yc, w_mg, w_mg, w_mg, b_mg, b_mg, b_mg, w_up, w_out)


def _rel_bias_table(rel_bias):
    n_h = rel_bias.shape[0]
    far = BAND - 1 - MAX_REL
    t = jnp.concatenate([rel_bias[:, MAX_REL - (CHUNK - 1):],
                         jnp.broadcast_to(rel_bias[:, -1:], (n_h, far))], axis=1)
    trev = t[:, ::-1]
    rows = [trev[:, CHUNK - 1 - qi:CHUNK - 1 - qi + BAND] for qi in range(CHUNK)]
    band = jnp.stack(rows, axis=1) * LOG2E
    span = LEFT_CHUNKS * CHUNK
    tiles = [jnp.pad(band, ((0, 0), (0, 0), (c * CHUNK, 2 * span - BAND - c * CHUNK)),
                     constant_values=NEG) for c in range(LEFT_CHUNKS)]
    return jnp.concatenate(tiles, axis=1)


def _layer(x2, batch, seq, layer, norm_g3, w16, wf_t, w_mg, b_mg, w_up16, w_out16, b_f, qk_norm_g,
           rel_bias):
    scale2 = HEAD_DIM ** -0.5 * LOG2E
    ones = jnp.ones((BRANCH_W,), f32)
    gq = lambda g: jnp.tile(g, N_HEADS)
    gains = jnp.stack([
        jnp.concatenate([gq(qk_norm_g[0]) * scale2, gq(qk_norm_g[1]), ones]),
        jnp.concatenate([gq(qk_norm_g[2]) * scale2, gq(qk_norm_g[3]), ones]),
        jnp.concatenate([ones * scale2, ones, ones]),
    ]).reshape(N_BRANCH, 1, 3 * BRANCH_W)
    bias_full = _rel_bias_table(rel_bias)

    qkv, gsilu, hn, f_t = _proj(x2, norm_g3, w16, wf_t, gains, layer)
    qkv3 = qkv.reshape(batch, seq, -1)
    gs3 = gsilu.reshape(batch, seq, -1)

    c = _forget_cumsum(f_t, b_f, batch, seq)
    cb = jnp.max(-c, axis=-1).reshape(-1)
    zb = (HEAD_DIM * scale2 * BF16_ROUND ** 2 * jnp.max(jnp.abs(qk_norm_g[0]))
          * jnp.max(jnp.abs(qk_norm_g[1]))).reshape(1)

    ya = _fox_attention(qkv3, c, cb, zb, gs3).reshape(batch * seq, BRANCH_W)
    yb = _chunk_attention(qkv3, bias_full, gs3).reshape(batch * seq, BRANCH_W)
    yc = _stick_attention(qkv3, gs3).reshape(batch * seq, BRANCH_W)
    return _tail(x2, hn, ya, yb, yc, w_mg, b_mg, w_up16, w_out16, layer)


def kernel(x, norm_g, w_in, b_f, b_gate, qk_norm_g, rel_bias, w_up, w_out):
    batch, seq, d = x.shape
    depth = norm_g.shape[0]
    x2 = x.reshape(batch * seq, d)
    qkvg = N_BRANCH * 4 * BRANCH_W
    w16 = w_in.astype(bf16)
    w_mg = w16[:, :, qkvg + N_HEADS:]
    wf = jnp.swapaxes(w_in[:, :, qkvg:qkvg + N_HEADS], 1, 2)
    wf_t = jnp.pad(wf, ((0, 0), (0, 8 - N_HEADS), (0, 0))).astype(bf16)
    b_mg = b_gate.reshape(depth, 1, N_BRANCH * d)
    w_up16 = w_up.astype(bf16)
    w_out16 = w_out.astype(bf16)
    norm_g3 = norm_g.reshape(depth, 1, d)
    for layer in range(depth):
        x2 = _layer(x2, batch, seq, layer, norm_g3, w16, wf_t, w_mg, b_mg, w_up16, w_out16,
                    b_f[layer], qk_norm_g[layer], rel_bias[layer])
    return x2.reshape(batch, seq, d)
```

```python
import math

import jax
import jax.numpy as jnp
from jax import lax
from jax.experimental import pallas as pl
from jax.experimental.pallas import tpu as pltpu

N_HEADS = 4
HEAD_DIM = 128
BRANCH_W = N_HEADS * HEAD_DIM
N_BRANCH = 3
CHUNK = 64
LEFT_CHUNKS = 8
BAND = (LEFT_CHUNKS + 1) * CHUNK
MAX_REL = 128
EPS = 1e-6
NEG = -1e30
LOG2E = math.log2(math.e)

LANES = 128
VMEM_LIMIT = 56 * 1024 * 1024

PROJ_TM = 512
ATT_TQ = 512
FOX_TK = 512
SB_TQ = 512
SB_TK = 256
SB_TILES = 2
CHUNK_SPANS = 4
CHUNK_SPLIT = 2
TAIL_TM = 512
TAIL_TN = 512

UNDERFLOW_LOG2 = 152.0
BF16_ROUND = 1.0 + 2.0 ** -8

f32 = jnp.float32
bf16 = jnp.bfloat16


def _cparams(sem):
    return pltpu.CompilerParams(dimension_semantics=sem, vmem_limit_bytes=VMEM_LIMIT)


def _dot_nt(a, b):
    return lax.dot_general(a, b, (((1,), (1,)), ((), ())), preferred_element_type=f32)


def _lane_tile(a, n):
    return a if n == 1 else jnp.concatenate([a] * n, axis=1)


def _proj_kernel(x_ref, g_ref, w_ref, wf_ref, gain_ref,
                 qkv_ref, gs_ref, hn_ref, ft_ref, hn_s):
    j = pl.program_id(1)

    @pl.when(j == 0)
    def _():
        x = x_ref[...]
        ms = jnp.mean(x * x, axis=-1, keepdims=True)
        hb = (x * lax.rsqrt(ms + EPS) * g_ref[...]).astype(bf16)
        hn_s[...] = hb
        hn_ref[...] = hb
        ft_ref[...] = _dot_nt(wf_ref[...], hb)

    hn = hn_s[...]
    gain = gain_ref[...]

    qk_normed = j < 2
    acc = jnp.dot(hn, w_ref[...], preferred_element_type=f32)
    for col in range(0, 2 * BRANCH_W, HEAD_DIM):
        a = acc[:, col:col + HEAD_DIM]
        rs = lax.rsqrt(jnp.mean(a * a, axis=-1, keepdims=True) + EPS)
        rs = jnp.where(qk_normed, rs, 1.0)
        qkv_ref[:, col:col + HEAD_DIM] = (a * rs * gain[:, col:col + HEAD_DIM]).astype(bf16)
    qkv_ref[:, 2 * BRANCH_W:] = acc[:, 2 * BRANCH_W:3 * BRANCH_W].astype(bf16)
    a = acc[:, 3 * BRANCH_W:]
    gs_ref[...] = a / (1.0 + jnp.exp(-a))


def _proj(x2, norm_g3, w16, wf_t, gains, layer):
    m, d = x2.shape
    tm = PROJ_TM

    return pl.pallas_call(
        _proj_kernel,
        grid=(m // tm, N_BRANCH),
        in_specs=[
            pl.BlockSpec((tm, d), lambda i, j: (i, 0)),
            pl.BlockSpec((None, 1, d), lambda i, j: (layer, 0, 0)),
            pl.BlockSpec((None, d, 4 * BRANCH_W), lambda i, j: (layer, 0, j)),
            pl.BlockSpec((None, 8, d), lambda i, j: (layer, 0, 0)),
            pl.BlockSpec((None, 1, 3 * BRANCH_W), lambda i, j: (j, 0, 0)),
        ],
        out_specs=[
            pl.BlockSpec((tm, 3 * BRANCH_W), lambda i, j: (i, j)),
            pl.BlockSpec((tm, BRANCH_W), lambda i, j: (i, j)),
            pl.BlockSpec((tm, d), lambda i, j: (i, 0)),
            pl.BlockSpec((8, tm), lambda i, j: (0, i)),
        ],
        out_shape=[
            jax.ShapeDtypeStruct((m, N_BRANCH * 3 * BRANCH_W), bf16),
            jax.ShapeDtypeStruct((m, N_BRANCH * BRANCH_W), f32),
            jax.ShapeDtypeStruct((m, d), bf16),
            jax.ShapeDtypeStruct((8, m), f32),
        ],
        scratch_shapes=[pltpu.VMEM((tm, d), bf16)],
        compiler_params=_cparams(("parallel", "arbitrary")),
        name="proj",
    )(x2, norm_g3, w16, wf_t, gains)


def _cumsum_kernel(bf_ref, f_ref, c_ref):
    h = pl.program_id(0)
    x = f_ref[0, 0] + bf_ref[h]
    lf = jnp.minimum(x, 0.0) - jnp.log(1.0 + jnp.exp(-jnp.abs(x)))
    r = lf.shape[0]
    hp = lax.Precision.HIGHEST
    row = lax.broadcasted_iota(jnp.int32, (LANES, LANES), 0)
    col = lax.broadcasted_iota(jnp.int32, (LANES, LANES), 1)
    upper = (row <= col).astype(f32)
    within = jnp.dot(lf, upper, precision=hp, preferred_element_type=f32)
    tot = jnp.broadcast_to(within[:, LANES - 1:LANES], (r, LANES))
    rr = lax.broadcasted_iota(jnp.int32, (r, r), 0)
    rc = lax.broadcasted_iota(jnp.int32, (r, r), 1)
    earlier = (rc < rr).astype(f32)
    offs = jnp.dot(earlier, tot, precision=hp, preferred_element_type=f32)
    c_ref[0, 0] = (within + offs) * LOG2E


def _forget_cumsum(f_t, b_f, batch, seq):
    r = seq // LANES
    f4 = f_t[:N_HEADS].reshape(N_HEADS, batch, r, LANES)
    c = pl.pallas_call(
        _cumsum_kernel,
        grid=(N_HEADS, batch),
        in_specs=[
            pl.BlockSpec(memory_space=pltpu.SMEM),
            pl.BlockSpec((1, 1, r, LANES), lambda h, b: (h, b, 0, 0)),
        ],
        out_specs=pl.BlockSpec((1, 1, r, LANES), lambda h, b: (h, b, 0, 0)),
        out_shape=jax.ShapeDtypeStruct((N_HEADS, batch, r, LANES), f32),
        compiler_params=_cparams(("parallel", "parallel")),
        name="forget_cumsum",
    )(b_f, f4)
    return c.reshape(N_HEADS, batch, seq // FOX_TK, FOX_TK)


def _fox_kernel(zb_ref, cb_ref, q_ref, k_ref, v_ref, c_ref, g_ref, o_ref, m_s, l_s, acc_s, s_s):
    b = pl.program_id(0)
    h = pl.program_id(1)
    i = pl.program_id(2)
    tq, tk = ATT_TQ, FOX_TK
    nt = pl.num_programs(2)
    rep = tk // LANES

    m_s[...] = jnp.full(m_s.shape, NEG, f32)
    l_s[...] = jnp.zeros(l_s.shape, f32)
    acc_s[...] = jnp.zeros(acc_s.shape, f32)

    q = q_ref[0]

    def logits(j):
        jj = jnp.maximum(j, 0)
        off = pl.multiple_of(jj * tk, tk)
        return _dot_nt(q, k_ref[0, pl.ds(off, tk), :]) - c_ref[0, 0, pl.ds(jj, 1), :]

    def step(j, s, masked):
        s_s[...] = logits(j - 1)
        off = pl.multiple_of(j * tk, tk)
        v = v_ref[0, pl.ds(off, tk), :]
        if masked:
            row = lax.broadcasted_iota(jnp.int32, (tq, tk), 0)
            col = lax.broadcasted_iota(jnp.int32, (tq, tk), 1)
            s = jnp.where(col <= row, s, NEG)
        m_old = m_s[...]
        m_new = jnp.maximum(m_old, jnp.max(s, axis=-1, keepdims=True))
        alpha = jnp.exp2(m_old - m_new)
        p = jnp.exp2(s - _lane_tile(m_new, rep))
        lsum = p[:, 0:LANES]
        for cblk in range(1, rep):
            lsum = lsum + p[:, cblk * LANES:(cblk + 1) * LANES]
        l_s[...] = alpha * l_s[...] + lsum
        acc_s[...] = alpha * acc_s[...] + jnp.dot(p.astype(bf16), v, preferred_element_type=f32)
        m_s[...] = m_new

    step(i, logits(i), True)

    thr = jnp.min(m_s[...]) - zb_ref[0] - UNDERFLOW_LOG2
    base = (h * pl.num_programs(0) + b) * nt

    def cond(j):
        return jnp.logical_and(j >= 0, cb_ref[base + jnp.maximum(j, 0)] > thr)

    def body(j):
        step(j, s_s[...], False)
        return j - 1

    lax.while_loop(cond, body, i - 1)
    l = jnp.sum(l_s[...], axis=-1, keepdims=True)
    o_ref[0] = (acc_s[...] / l * g_ref[0]).astype(bf16)


def _fox_attention(qkv3, c, cb, zb, gs3):
    batch, seq, _ = qkv3.shape
    tq = ATT_TQ
    assert FOX_TK == tq
    return pl.pallas_call(
        _fox_kernel,
        grid=(batch, N_HEADS, seq // tq),
        in_specs=[
            pl.BlockSpec(memory_space=pltpu.SMEM),
            pl.BlockSpec(memory_space=pltpu.SMEM),
            pl.BlockSpec((1, tq, HEAD_DIM), lambda b, h, i: (b, i, h)),
            pl.BlockSpec((1, seq, HEAD_DIM), lambda b, h, i: (b, 0, N_HEADS + h)),
            pl.BlockSpec((1, seq, HEAD_DIM), lambda b, h, i: (b, 0, 2 * N_HEADS + h)),
            pl.BlockSpec((1, 1, seq // FOX_TK, FOX_TK), lambda b, h, i: (h, b, 0, 0)),
            pl.BlockSpec((1, tq, HEAD_DIM), lambda b, h, i: (b, i, h)),
        ],
        out_specs=pl.BlockSpec((1, tq, HEAD_DIM), lambda b, h, i: (b, i, h)),
        out_shape=jax.ShapeDtypeStruct((batch, seq, BRANCH_W), bf16),
        scratch_shapes=[
            pltpu.VMEM((tq, LANES), f32),
            pltpu.VMEM((tq, LANES), f32),
            pltpu.VMEM((tq, HEAD_DIM), f32),
            pltpu.VMEM((tq, FOX_TK), f32),
        ],
        compiler_params=_cparams(("parallel", "parallel", "arbitrary")),
        name="fox_attention",
    )(zb, cb, qkv3, qkv3, qkv3, c, gs3)


def _chunk_kernel(q_ref, kp_ref, kc_ref, vp_ref, vc_ref, bias_ref, g_ref, o_ref):
    i = pl.program_id(2)
    span = LEFT_CHUNKS * CHUNK
    rows = span // CHUNK_SPLIT
    for t in range(CHUNK_SPANS):
        t0 = t * span
        if t == 0:
            kprev, vprev = kp_ref.at[0], vp_ref.at[0]
        else:
            kprev, vprev = kc_ref.at[0, t0 - span:t0], vc_ref.at[0, t0 - span:t0]
        kcur, vcur = kc_ref.at[0, t0:t0 + span], vc_ref.at[0, t0:t0 + span]
        for r0 in range(0, span, rows):
            q = q_ref[0, t0 + r0:t0 + r0 + rows, :]
            s_prev = _dot_nt(q, kprev[r0:span, :]) + bias_ref[0, r0:r0 + rows, r0:span]
            s_cur = (_dot_nt(q, kcur[0:r0 + rows, :])
                     + bias_ref[0, r0:r0 + rows, span:span + r0 + rows])
            if t == 0:
                s_prev = jnp.where(i > 0, s_prev, NEG)
            m = jnp.maximum(jnp.max(s_prev, axis=-1, keepdims=True),
                            jnp.max(s_cur, axis=-1, keepdims=True))
            p_prev = jnp.exp2(s_prev - m)
            p_cur = jnp.exp2(s_cur - m)
            l = jnp.sum(p_prev, axis=-1, keepdims=True) + jnp.sum(p_cur, axis=-1, keepdims=True)
            o = (jnp.dot(p_prev.astype(bf16), vprev[r0:span, :], preferred_element_type=f32)
                 + jnp.dot(p_cur.astype(bf16), vcur[0:r0 + rows, :], preferred_element_type=f32))
            sl = slice(t0 + r0, t0 + r0 + rows)
            o_ref[0, sl, :] = (o / l * g_ref[0, sl, :]).astype(bf16)


def _chunk_attention(qkv3, bias_full, gs3):
    batch, seq, _ = qkv3.shape
    span = LEFT_CHUNKS * CHUNK
    tq = CHUNK_SPANS * span
    base = 3 * N_HEADS

    def prev(b, h, i, seg):
        return (b, jnp.maximum(CHUNK_SPANS * i - 1, 0), base + seg * N_HEADS + h)

    return pl.pallas_call(
        _chunk_kernel,
        grid=(batch, N_HEADS, seq // tq),
        in_specs=[
            pl.BlockSpec((1, tq, HEAD_DIM), lambda b, h, i: (b, i, base + h)),
            pl.BlockSpec((1, span, HEAD_DIM), lambda b, h, i: prev(b, h, i, 1)),
            pl.BlockSpec((1, tq, HEAD_DIM), lambda b, h, i: (b, i, base + N_HEADS + h)),
            pl.BlockSpec((1, span, HEAD_DIM), lambda b, h, i: prev(b, h, i, 2)),
            pl.BlockSpec((1, tq, HEAD_DIM), lambda b, h, i: (b, i, base + 2 * N_HEADS + h)),
            pl.BlockSpec((1, span, 2 * span), lambda b, h, i: (h, 0, 0)),
            pl.BlockSpec((1, tq, HEAD_DIM), lambda b, h, i: (b, i, N_HEADS + h)),
        ],
        out_specs=pl.BlockSpec((1, tq, HEAD_DIM), lambda b, h, i: (b, i, h)),
        out_shape=jax.ShapeDtypeStruct((batch, seq, BRANCH_W), bf16),
        compiler_params=_cparams(("parallel", "parallel", "arbitrary")),
        name="chunk_attention",
    )(qkv3, qkv3, qkv3, qkv3, qkv3, bias_full, gs3)


def _stick_kernel(q_ref, k_ref, v_ref, g_ref, o_ref, carry_s, acc_s):
    tq, tk = SB_TQ, SB_TK
    per = tq // tk
    rep = tk // LANES

    kr = lax.broadcasted_iota(jnp.int32, (2 * tk, tk), 0)
    kc = lax.broadcasted_iota(jnp.int32, (2 * tk, tk), 1)
    kr = jnp.where(kr >= tk, kr - tk, kr)
    suffix = (kr >= kc).astype(bf16)

    def scores(q, i, j, masked):
        jj = jnp.maximum(j, 0)
        off = pl.multiple_of(jj * tk, tk)
        z = _dot_nt(q, k_ref[0, pl.ds(off, tk), :])
        p = jnp.maximum(z, 0.0) + jnp.log2(1.0 + jnp.exp2(-jnp.abs(z)))
        if masked:
            row = i * tq + lax.broadcasted_iota(jnp.int32, (tq, tk), 0)
            col = j * tk + lax.broadcasted_iota(jnp.int32, (tq, tk), 1)
            keep = col < row
        else:
            keep = j >= 0
        p = jnp.where(keep, p, 0.0)
        hi = p.astype(bf16)
        lo = (p - hi.astype(f32)).astype(bf16)
        rp = jnp.dot(jnp.concatenate([hi, lo], axis=1), suffix, preferred_element_type=f32)
        return z - rp, rp[:, 0:1], keep, off

    def weigh(zr, tot, keep, off, carry):
        a = jnp.where(keep, jnp.exp2(zr - _lane_tile(carry, rep)), 0.0)
        pv = jnp.dot(a.astype(bf16), v_ref[0, pl.ds(off, tk), :], preferred_element_type=f32)
        return pv, carry + jnp.broadcast_to(tot, (tq, LANES))

    tiles = [pl.program_id(2) * SB_TILES + u for u in range(SB_TILES)]
    qs = [q_ref[0, u * tq:(u + 1) * tq, :] for u in range(SB_TILES)]
    static = [[scores(qs[u], i, i * per + per - 1 - d, d < per) for d in range(per + 1)]
              for u, i in enumerate(tiles)]
    cmins = []
    for u in range(SB_TILES):
        carry = jnp.zeros((tq, LANES), f32)
        acc = jnp.zeros((tq, HEAD_DIM), f32)
        for zr, tot, keep, off in static[u]:
            pv, carry = weigh(zr, tot, keep, off, carry)
            acc = acc + pv
        carry_s[u] = carry
        acc_s[u] = acc
        cmins.append(jnp.min(carry))

    def cond(state):
        j, cmin = state
        return jnp.logical_and(j >= 0, cmin < UNDERFLOW_LOG2)

    for u, i in enumerate(tiles):
        def body(state, u=u, i=i):
            j, _ = state
            zr, tot, keep, off = scores(qs[u], i, j, False)
            pv, carry = weigh(zr, tot, keep, off, carry_s[u])
            acc_s[u] += pv
            carry_s[u] = carry
            return j - 1, jnp.min(carry)

        lax.while_loop(cond, body, (i * per - 2, cmins[u]))
        sl = slice(u * tq, (u + 1) * tq)
        o_ref[0, sl, :] = (acc_s[u] * g_ref[0, sl, :]).astype(bf16)


def _stick_attention(qkv3, gs3):
    batch, seq, _ = qkv3.shape
    tq = SB_TQ * SB_TILES
    base = 6 * N_HEADS
    return pl.pallas_call(
        _stick_kernel,
        grid=(batch, N_HEADS, seq // tq),
        in_specs=[
            pl.BlockSpec((1, tq, HEAD_DIM), lambda b, h, i: (b, i, base + h)),
            pl.BlockSpec((1, seq, HEAD_DIM), lambda b, h, i: (b, 0, base + N_HEADS + h)),
            pl.BlockSpec((1, seq, HEAD_DIM), lambda b, h, i: (b, 0, base + 2 * N_HEADS + h)),
            pl.BlockSpec((1, tq, HEAD_DIM), lambda b, h, i: (b, i, 2 * N_HEADS + h)),
        ],
        out_specs=pl.BlockSpec((1, tq, HEAD_DIM), lambda b, h, i: (b, i, h)),
        out_shape=jax.ShapeDtypeStruct((batch, seq, BRANCH_W), bf16),
        scratch_shapes=[
            pltpu.VMEM((SB_TILES, SB_TQ, LANES), f32),
            pltpu.VMEM((SB_TILES, SB_TQ, HEAD_DIM), f32),
        ],
        compiler_params=_cparams(("parallel", "parallel", "arbitrary")),
        name="stick_attention",
    )(qkv3, qkv3, qkv3, gs3)


def _tail_kernel(x_ref, hn_ref, ya_ref, yb_ref, yc_ref, wg0_ref, wg1_ref, wg2_ref,
                 bg0_ref, bg1_ref, bg2_ref, wup_ref, wout_ref, o_ref):
    j = pl.program_id(1)

    @pl.when(j == 0)
    def _():
        o_ref[...] = x_ref[...]

    hn = hn_ref[...]
    merged = None
    branches = ((ya_ref, wg0_ref, bg0_ref), (yb_ref, wg1_ref, bg1_ref), (yc_ref, wg2_ref, bg2_ref))
    for n, (y_ref, wg_ref, bg_ref) in enumerate(branches):
        gl = jnp.dot(hn, wg_ref[...], preferred_element_type=f32) + bg_ref[...]
        gate = 1.0 / (1.0 + jnp.exp(-gl))
        up = jnp.dot(y_ref[...], wup_ref[n], preferred_element_type=f32)
        merged = gate * up if merged is None else merged + gate * up
    o_ref[...] += jnp.dot(merged.astype(bf16), wout_ref[...], preferred_element_type=f32)


def _tail(x2, hn, ya, yb, yc, w_mg, b_mg, w_up, w_out, layer):
    m, d = x2.shape
    tm, tn = TAIL_TM, TAIL_TN
    nj = d // tn

    def wg_spec(n):
        return pl.BlockSpec((None, d, tn), lambda i, j: (layer, 0, n * nj + j))

    def bg_spec(n):
        return pl.BlockSpec((None, 1, tn), lambda i, j: (layer, 0, n * nj + j))

    y_spec = pl.BlockSpec((tm, BRANCH_W), lambda i, j: (i, 0))
    return pl.pallas_call(
        _tail_kernel,
        grid=(m // tm, nj),
        in_specs=[
            pl.BlockSpec((tm, d), lambda i, j: (i, 0)),
            pl.BlockSpec((tm, d), lambda i, j: (i, 0)),
            y_spec, y_spec, y_spec,
            wg_spec(0), wg_spec(1), wg_spec(2),
            bg_spec(0), bg_spec(1), bg_spec(2),
            pl.BlockSpec((None, N_BRANCH, BRANCH_W, tn), lambda i, j: (layer, 0, 0, j)),
            pl.BlockSpec((None, tn, d), lambda i, j: (layer, j, 0)),
        ],
        out_specs=pl.BlockSpec((tm, d), lambda i, j: (i, 0)),
        out_shape=jax.ShapeDtypeStruct((m, d), f32),
        compiler_params=_cparams(("parallel", "arbitrary")),
        name="merge_tail",
    )(x2, hn, ya, yb, yc, w_mg, w_mg, w_mg, b_mg, b_mg, b_mg, w_up, w_out)


def _rel_bias_table(rel_bias):
    n_h = rel_bias.shape[0]
    far = BAND - 1 - MAX_REL
    t = jnp.concatenate([rel_bias[:, MAX_REL - (CHUNK - 1):],
                         jnp.broadcast_to(rel_bias[:, -1:], (n_h, far))], axis=1)
    trev = t[:, ::-1]
    rows = [trev[:, CHUNK - 1 - qi:CHUNK - 1 - qi + BAND] for qi in range(CHUNK)]
    band = jnp.stack(rows, axis=1) * LOG2E
    span = LEFT_CHUNKS * CHUNK
    tiles = [jnp.pad(band, ((0, 0), (0, 0), (c * CHUNK, 2 * span - BAND - c * CHUNK)),
                     constant_values=NEG) for c in range(LEFT_CHUNKS)]
    return jnp.concatenate(tiles, axis=1)


def _layer(x2, batch, seq, layer, norm_g3, w16, wf_t, w_mg, b_mg, w_up16, w_out16, b_f, qk_norm_g,
           rel_bias):
    scale2 = HEAD_DIM ** -0.5 * LOG2E
    ones = jnp.ones((BRANCH_W,), f32)
    gq = lambda g: jnp.tile(g, N_HEADS)
    gains = jnp.stack([
        jnp.concatenate([gq(qk_norm_g[0]) * scale2, gq(qk_norm_g[1]), ones]),
        jnp.concatenate([gq(qk_norm_g[2]) * scale2, gq(qk_norm_g[3]), ones]),
        jnp.concatenate([ones * scale2, ones, ones]),
    ]).reshape(N_BRANCH, 1, 3 * BRANCH_W)
    bias_full = _rel_bias_table(rel_bias)

    qkv, gsilu, hn, f_t = _proj(x2, norm_g3, w16, wf_t, gains, layer)
    qkv3 = qkv.reshape(batch, seq, -1)
    gs3 = gsilu.reshape(batch, seq, -1)

    c = _forget_cumsum(f_t, b_f, batch, seq)
    cb = jnp.max(-c, axis=-1).reshape(-1)
    zb = (HEAD_DIM * scale2 * BF16_ROUND ** 2 * jnp.max(jnp.abs(qk_norm_g[0]))
          * jnp.max(jnp.abs(qk_norm_g[1]))).reshape(1)

    ya = _fox_attention(qkv3, c, cb, zb, gs3).reshape(batch * seq, BRANCH_W)
    yb = _chunk_attention(qkv3, bias_full, gs3).reshape(batch * seq, BRANCH_W)
    yc = _stick_attention(qkv3, gs3).reshape(batch * seq, BRANCH_W)
    return _tail(x2, hn, ya, yb, yc, w_mg, b_mg, w_up16, w_out16, layer)


def kernel(x, norm_g, w_in, b_f, b_gate, qk_norm_g, rel_bias, w_up, w_out):
    batch, seq, d = x.shape
    depth = norm_g.shape[0]
    x2 = x.reshape(batch * seq, d)
    qkvg = N_BRANCH * 4 * BRANCH_W
    w16 = w_in.astype(bf16)
    w_mg = w16[:, :, qkvg + N_HEADS:]
    wf = jnp.swapaxes(w_in[:, :, qkvg:qkvg + N_HEADS], 1, 2)
    wf_t = jnp.pad(wf, ((0, 0), (0, 8 - N_HEADS), (0, 0))).astype(bf16)
    b_mg = b_gate.reshape(depth, 1, N_BRANCH * d)
    w_up16 = w_up.astype(bf16)
    w_out16 = w_out.astype(bf16)
    norm_g3 = norm_g.reshape(depth, 1, d)
    for layer in range(depth):
        x2 = _layer(x2, batch, seq, layer, norm_g3, w16, wf_t, w_mg, b_mg, w_up16, w_out16,
                    b_f[layer], qk_norm_g[layer], rel_bias[layer])
    return x2.reshape(batch, seq, d)
```
